```python
import functools
import jax, jax.numpy as jnp
from jax import lax
import numpy as np

D_MODEL = 1024
BATCH = 8
SEQ = 2048
DEPTH = 2
DEC_BATCH = 32
DEC_SEQ = 1
PAST_LEN = 16384
PAGE_SIZE = 128

N_GM_LAYERS = (DEPTH + 1) // 2
N_NS_LAYERS = DEPTH // 2
HEAD_DIM = 64
GLA_HEADS = 4
GLA_DK = 64
GLA_DV = 128
GLA_GATE_RANK = 16
GLA_TAU = 16.0
GLA_CHUNK = 16
MOBA_HEADS = 8
MOBA_BLOCK = 256
MOBA_TOPK = 3
NSA_HEADS = 16
NSA_KV_HEADS = 4
NSA_GROUP = NSA_HEADS // NSA_KV_HEADS
NSA_CMP_LEN = 32
NSA_CMP_STRIDE = 16
NSA_CMP_HIDDEN = 64
NSA_SLC_BLOCK = 64
NSA_SLC_TOPN = 16
NSA_INIT_BLOCKS = 1
NSA_LOCAL_BLOCKS = 2
NSA_WINDOW = 512
FFN_DIM = 2816
N_EXPERTS = 8
TOP_K = 2
EXPERT_DIM = 3584
MOE_BLOCK = 128
QUERY_BLOCK = 16
RMS_EPS = 1e-6
NEG_INF = -1e30
FORCE_SCORE = 1e9

GM_COLS = (GLA_HEADS * GLA_DK, GLA_HEADS * GLA_DK, GLA_HEADS * GLA_DV, GLA_GATE_RANK, GLA_HEADS * GLA_DV,
           MOBA_HEADS * HEAD_DIM, MOBA_HEADS * HEAD_DIM, MOBA_HEADS * HEAD_DIM)
GM_MIX = GLA_HEADS * GLA_DV + MOBA_HEADS * HEAD_DIM
NS_COLS = (NSA_HEADS * HEAD_DIM,) + (NSA_KV_HEADS * HEAD_DIM,) * 6 + (NSA_HEADS * 3,)
NS_MIX = NSA_HEADS * HEAD_DIM

kernel_name = 'hybrid_gla_moba_nsa_moe_step'


def _rms(x, g):
    xf = x.astype(jnp.float32)
    y = xf * lax.rsqrt(jnp.mean(xf * xf, axis=-1, keepdims=True) + RMS_EPS)
    return (y * g.astype(jnp.float32)).astype(x.dtype)


def _split(a, sizes):
    return jnp.split(a, np.cumsum(sizes)[:-1].tolist(), axis=-1)


def _alibi_slopes(n):
    return 2.0 ** (-8.0 * jnp.arange(1, n + 1, dtype=jnp.float32) / n)


def _swiglu(h, w_gate, w_up, w_down):
    return (jax.nn.silu(h @ w_gate) * (h @ w_up)) @ w_down


def _gather_pages(pool, page_table):
    g = pool[page_table]
    return g.reshape(g.shape[0], -1, *pool.shape[2:])


def _fetch_dense(src, pos):
    b = jnp.arange(src.shape[0])[:, None, None, None]
    g = jnp.arange(src.shape[2])[None, None, :, None]
    return src[b, jnp.clip(pos, 0, src.shape[1] - 1), g]


def _fetch_paged(pool, page_table, new, pos):
    ps = pool.shape[1]
    past = page_table.shape[1] * ps
    b = jnp.arange(page_table.shape[0])[:, None, None, None]
    g = jnp.arange(pool.shape[2])[None, None, :, None]
    pc = jnp.clip(pos, 0, past - 1)
    phys = page_table[b, pc // ps]
    old = pool[phys, pc % ps, g]
    fresh = _fetch_dense(new, pos - past)
    return jnp.where((pos < past)[..., None], old, fresh)


def _sweep(fn, pos0, *xs):
    B, T = xs[0].shape[:2]
    qc = min(QUERY_BLOCK, T)
    n = -(-T // qc)
    pad = n * qc - T

    def prep(a):
        a = jnp.pad(a, [(0, 0), (0, pad)] + [(0, 0)] * (a.ndim - 2))
        return jnp.moveaxis(a.reshape(B, n, qc, *a.shape[2:]), 1, 0)

    pos = (pos0 + jnp.arange(n * qc, dtype=jnp.int32)).reshape(n, qc)
    out = lax.map(lambda a: fn(a[0], *a[1:]), (pos, *[prep(a) for a in xs]))
    return jnp.moveaxis(out, 0, 1).reshape(B, n * qc, *out.shape[3:])[:, :T]


def _attend(q, k, v, kpos, valid, qpos, slopes, shared):
    kspec = 'bngd' if shared else 'bqgnd'
    s = jnp.einsum('bqgrd,' + kspec + '->bqgrn', q, k, preferred_element_type=jnp.float32)
    dist = (qpos[None, :, None, None] - kpos).astype(jnp.float32)
    s = s - slopes[None, None, :, :, None] * dist[..., None, :]
    mask = valid[..., None, :]
    s = jnp.where(mask, s, NEG_INF)
    prob = jax.nn.softmax(s, axis=-1) * mask
    o = jnp.einsum('bqgrn,' + kspec + '->bqgrd', prob.astype(v.dtype), v)
    return o, prob


def _gla_scan(q, k, v, log_a, s0):
    B, T = q.shape[:2]
    C = GLA_CHUNK
    n = -(-T // C)
    pad = n * C - T

    def prep(a):
        a = jnp.pad(a.astype(jnp.float32), ((0, 0), (0, pad), (0, 0), (0, 0)))
        return jnp.moveaxis(a.reshape(B, n, C, *a.shape[2:]), 1, 0)

    causal = jnp.tril(jnp.ones((C, C), dtype=bool))

    def step(s, inp):
        qc, kc, vc, gc = inp
        b = jnp.cumsum(gc, axis=1)
        inter = jnp.einsum('bchk,bhkv->bchv', qc * jnp.exp(b), s)
        diff = b[:, :, None] - b[:, None, :]
        decay = jnp.exp(jnp.where(causal[None, :, :, None, None], diff, -jnp.inf))
        scores = jnp.einsum('bthk,bshk,btshk->bths', qc, kc, decay)
        intra = jnp.einsum('bths,bshv->bthv', scores, vc)
        b_last = b[:, -1]
        s_new = jnp.exp(b_last)[..., None] * s + jnp.einsum('bshk,bshv->bhkv', kc * jnp.exp(b_last[:, None] - b), vc)
        return s_new, inter + intra

    s_T, o = lax.scan(step, s0.astype(jnp.float32), tuple(prep(a) for a in (q, k, v, log_a)))
    o = jnp.moveaxis(o, 0, 1).reshape(B, n * C, *o.shape[3:])[:, :T]
    return o, s_T


def _moba_attend(q, pos0, k_all, fetch_k, fetch_v):
    B, L, H, Dh = k_all.shape
    n_full = L // MOBA_BLOCK
    n_blk = max(n_full, MOBA_TOPK)
    kmean = k_all[:, :n_full * MOBA_BLOCK].astype(jnp.float32).reshape(B, n_full, MOBA_BLOCK, H, Dh).mean(axis=2)
    kmean = jnp.pad(kmean, ((0, 0), (0, n_blk - n_full), (0, 0), (0, 0)))
    slopes = _alibi_slopes(H)[:, None]
    blk = jnp.arange(n_blk)
    offs = jnp.arange(MOBA_BLOCK)

    def chunk(p, qc):
        Q = p.shape[0]
        cur = p // MOBA_BLOCK
        gate = jnp.einsum('bqhd,bnhd->bqhn', qc.astype(jnp.float32), kmean)
        gate = jnp.where((blk[None, :] < cur[:, None])[None, :, None, :], gate, NEG_INF)
        _, top = lax.top_k(gate, MOBA_TOPK)
        cur_b = jnp.broadcast_to(cur[None, :, None, None], (B, Q, H, 1))
        blocks = jnp.concatenate([top, cur_b], axis=-1)
        ok = jnp.concatenate([top < cur_b, jnp.ones_like(cur_b, dtype=bool)], axis=-1)
        kpos = (blocks[..., None] * MOBA_BLOCK + offs).reshape(B, Q, H, -1)
        valid = jnp.repeat(ok, MOBA_BLOCK, axis=-1) & (kpos <= p[None, :, None, None])
        o, _ = _attend(qc[:, :, :, None], fetch_k(kpos), fetch_v(kpos), kpos, valid, p, slopes, False)
        return o[:, :, :, 0]

    return _sweep(chunk, pos0, q)


def _compress(raw, pe, w1, b1, w2, b2):
    B, L, G, Dh = raw.shape
    n_cmp = max((L - NSA_CMP_LEN) // NSA_CMP_STRIDE + 1, 1)
    ratio = NSA_CMP_LEN // NSA_CMP_STRIDE
    n_chunk = n_cmp + ratio - 1
    chunks = raw[:, :n_chunk * NSA_CMP_STRIDE].reshape(B, n_chunk, NSA_CMP_STRIDE, G, Dh)
    w1r = w1.reshape(ratio, NSA_CMP_STRIDE, Dh, -1)
    part = jnp.einsum('bcsgd,rsdh->rbcgh', chunks, w1r)
    hid = part[0, :, :n_cmp]
    for r in range(1, ratio):
        hid = hid + part[r, :, r:r + n_cmp]
    hid = hid + (pe.reshape(-1) @ w1 + b1)
    return jnp.einsum('bcgh,hd->bcgd', jax.nn.silu(hid), w2) + b2


def _nsa_attend(q, gates, pos0, kc, vc, fetch_k, fetch_v, kw, vw, kw_pos0, n_keys):
    B, T, G, R, Dh = q.shape
    n_cmp = kc.shape[1]
    c_start = jnp.arange(n_cmp) * NSA_CMP_STRIDE
    c_end = c_start + NSA_CMP_LEN - 1
    n_blk = max(-(-n_keys // NSA_SLC_BLOCK), NSA_SLC_TOPN)
    blk = jnp.arange(n_blk)
    b_start = blk * NSA_SLC_BLOCK
    overlap = ((c_start[:, None] < b_start[None, :] + NSA_SLC_BLOCK) &
               (c_end[:, None] >= b_start[None, :])).astype(jnp.float32)
    slopes = _alibi_slopes(G * R).reshape(G, R)
    offs = jnp.arange(NSA_SLC_BLOCK)

    def chunk(p, qc, gc):
        Q = p.shape[0]
        valid_c = (c_end[None, :] <= p[:, None])[None, :, None, :]
        o_c, p_c = _attend(qc, kc, vc, c_end[None, None, None, :], valid_c, p, slopes, True)
        imp = jnp.einsum('bqgrc,cn->bqgn', p_c, overlap)
        cur = p // NSA_SLC_BLOCK
        allowed = blk[None, :] <= cur[:, None]
        forced = (blk[None, :] < NSA_INIT_BLOCKS) | (cur[:, None] - blk[None, :] < NSA_LOCAL_BLOCKS)
        imp = jnp.where((allowed & forced)[None, :, None, :], FORCE_SCORE,
                        jnp.where(allowed[None, :, None, :], imp, NEG_INF))
        _, sel = lax.top_k(imp, NSA_SLC_TOPN)
        spos = (sel[..., None] * NSA_SLC_BLOCK + offs).reshape(B, Q, G, -1)
        valid_s = jnp.repeat(sel <= cur[None, :, None, None], NSA_SLC_BLOCK, axis=-1) & (spos <= p[None, :, None, None])
        o_s, _ = _attend(qc, fetch_k(spos), fetch_v(spos), spos, valid_s, p, slopes, False)
        n_w = NSA_WINDOW + Q
        start = p[0] - NSA_WINDOW - kw_pos0
        kwin = lax.dynamic_slice_in_dim(kw, start, n_w, axis=1)
        vwin = lax.dynamic_slice_in_dim(vw, start, n_w, axis=1)
        wpos = kw_pos0 + start + jnp.arange(n_w)
        dist = p[:, None] - wpos[None, :]
        valid_w = ((wpos[None, :] >= 0) & (dist >= 0) & (dist <= NSA_WINDOW))[None, :, None, :]
        o_w, _ = _attend(qc, kwin, vwin, wpos[None, None, None, :], valid_w, p, slopes, True)
        g = gc.astype(jnp.float32)
        o = g[..., 0:1] * o_c + g[..., 1:2] * o_s + g[..., 2:3] * o_w
        return o.astype(q.dtype)

    return _sweep(chunk, pos0, q, gates)


def _moe(h, router, w_gate, w_up, w_down):
    B, T, D = h.shape
    n_tok = B * T
    n_asg = n_tok * TOP_K
    x = h.reshape(n_tok, D)
    logits = jnp.einsum('nd,de->ne', x, router, preferred_element_type=jnp.float32)
    top_val, top_idx = lax.top_k(logits, TOP_K)
    gate = jax.nn.softmax(top_val, axis=-1).reshape(n_asg)
    expert = top_idx.reshape(n_asg)
    token = jnp.repeat(jnp.arange(n_tok, dtype=jnp.int32), TOP_K)
    order = jnp.argsort(expert)
    e_sorted = expert[order]
    counts = jnp.bincount(expert, length=N_EXPERTS)
    starts = jnp.cumsum(counts) - counts
    rows = min(MOE_BLOCK, n_asg)
    padded = (counts + rows - 1) // rows * rows
    p_end = jnp.cumsum(padded)
    slot = (p_end - padded)[e_sorted] + jnp.arange(n_asg) - starts[e_sorted]
    n_blocks = -(-n_asg // rows) + N_EXPERTS
    tok_of_slot = jnp.full((n_blocks * rows,), n_tok, jnp.int32).at[slot].set(token[order])
    block_expert = jnp.minimum(jnp.searchsorted(p_end, jnp.arange(n_blocks) * rows, side='right'), N_EXPERTS - 1)
    x_ext = jnp.concatenate([x, jnp.zeros((1, D), x.dtype)], axis=0)

    def expert_block(args):
        tok, e = args
        xb = x_ext[tok]
        return (jax.nn.silu(xb @ w_gate[e]) * (xb @ w_up[e])) @ w_down[e]

    y_slot = lax.map(expert_block, (tok_of_slot.reshape(n_blocks, rows), block_expert)).reshape(-1, D)
    contrib = y_slot[slot] * gate[order][:, None].astype(h.dtype)
    out = jax.ops.segment_sum(contrib, token[order], num_segments=n_tok)
    return out.reshape(B, T, D)


def _gm_sources_prompt(km, vm):
    return km, functools.partial(_fetch_dense, km), functools.partial(_fetch_dense, vm)


def _gm_sources_paged(pool_k, pool_v, page_table):
    def sources(km, vm):
        k_all = jnp.concatenate([_gather_pages(pool_k, page_table).astype(km.dtype), km], axis=1)
        return k_all, functools.partial(_fetch_dense, k_all), functools.partial(_fetch_paged, pool_v, page_table, vm)
    return sources


def _ns_sources_prompt(kc, vc, ks, vs, kw, vw):
    return kc, vc, functools.partial(_fetch_dense, ks), functools.partial(_fetch_dense, vs), kw, vw, 0


def _ns_sources_paged(pool_ck, pool_cv, pool_sk, pool_sv, win_k, win_v, page_table):
    past = page_table.shape[1] * pool_ck.shape[1]

    def sources(kc, vc, ks, vs, kw, vw):
        kc_all = jnp.concatenate([_gather_pages(pool_ck, page_table).astype(kc.dtype), kc], axis=1)
        vc_all = jnp.concatenate([_gather_pages(pool_cv, page_table).astype(vc.dtype), vc], axis=1)
        return (kc_all, vc_all,
                functools.partial(_fetch_paged, pool_sk, page_table, ks),
                functools.partial(_fetch_paged, pool_sv, page_table, vs),
                jnp.concatenate([win_k.astype(kw.dtype), kw], axis=1),
                jnp.concatenate([win_v.astype(vw.dtype), vw], axis=1),
                past - win_k.shape[1])
    return sources


def _mix_gm(h, pos0, s0, sources, w_in, gate_w, gate_b, out_norm, q_norm, k_norm, w_out):
    B, T, _ = h.shape
    qg, kg, vg, ga, rg, qm, km, vm = _split(h @ w_in, GM_COLS)
    qg = qg.reshape(B, T, GLA_HEADS, GLA_DK) * GLA_DK ** -0.5
    kg = kg.reshape(B, T, GLA_HEADS, GLA_DK)
    vg = vg.reshape(B, T, GLA_HEADS, GLA_DV)
    log_a = (jax.nn.log_sigmoid((ga @ gate_w + gate_b).astype(jnp.float32)) / GLA_TAU).reshape(B, T, GLA_HEADS, GLA_DK)
    o_g, s_T = _gla_scan(qg, kg, vg, log_a, s0)
    o_g = _rms(o_g, out_norm.reshape(GLA_HEADS, GLA_DV)).astype(h.dtype).reshape(B, T, -1) * jax.nn.silu(rg)
    qm = _rms(qm.reshape(B, T, MOBA_HEADS, HEAD_DIM), q_norm) * HEAD_DIM ** -0.5
    km = _rms(km.reshape(B, T, MOBA_HEADS, HEAD_DIM), k_norm)
    vm = vm.reshape(B, T, MOBA_HEADS, HEAD_DIM)
    k_all, fetch_k, fetch_v = sources(km, vm)
    o_m = _moba_attend(qm, pos0, k_all, fetch_k, fetch_v).reshape(B, T, -1).astype(h.dtype)
    y = jnp.concatenate([o_g, o_m], axis=-1) @ w_out
    return y, s_T, km, vm


def _mix_ns(h, pos0, sources, w_in, gate_b, q_norm, kc_norm, ks_norm, kw_norm, cmpk, cmpv, w_out):
    B, T, _ = h.shape
    q, kc, vc, ks, vs, kw, vw, g = _split(h @ w_in, NS_COLS)
    q = (_rms(q.reshape(B, T, NSA_HEADS, HEAD_DIM), q_norm) * HEAD_DIM ** -0.5).reshape(B, T, NSA_KV_HEADS, NSA_GROUP, HEAD_DIM)
    kvs = (B, T, NSA_KV_HEADS, HEAD_DIM)
    kc, vc = kc.reshape(kvs), vc.reshape(kvs)
    ks, vs = _rms(ks.reshape(kvs), ks_norm), vs.reshape(kvs)
    kw, vw = _rms(kw.reshape(kvs), kw_norm), vw.reshape(kvs)
    gates = jax.nn.sigmoid((g + gate_b).astype(jnp.float32)).reshape(B, T, NSA_KV_HEADS, NSA_GROUP, 3)
    kc_all, vc_all, fetch_k, fetch_v, kw_rows, vw_rows, rows_pos0 = sources(kc, vc, ks, vs, kw, vw)
    kcmp = _rms(_compress(kc_all, *cmpk), kc_norm)
    vcmp = _compress(vc_all, *cmpv)
    pad = ((0, 0), (NSA_WINDOW, QUERY_BLOCK), (0, 0), (0, 0))
    o = _nsa_attend(q, gates, pos0, kcmp, vcmp, fetch_k, fetch_v, jnp.pad(kw_rows, pad), jnp.pad(vw_rows, pad),
                    rows_pos0 - NSA_WINDOW, kc_all.shape[1])
    y = o.reshape(B, T, -1) @ w_out
    n_win = min(NSA_WINDOW, kw_rows.shape[1])
    return y, kc, vc, ks, vs, kw_rows[:, -n_win:], vw_rows[:, -n_win:]


def setup_inputs(seed: int = 0) -> dict:
    key = jax.random.key(seed)
    keys = iter(jax.random.split(key, 64))
    f32 = jnp.float32

    def nrm(shape, scale):
        return jax.random.normal(next(keys), shape, f32) * scale

    def gain(shape):
        return 1.0 + 0.01 * jax.random.normal(next(keys), shape, f32)

    n_pages = PAST_LEN // PAGE_SIZE
    n_pool = (DEC_BATCH * n_pages * 5) // 4
    win_rows = min(NSA_WINDOW, PAST_LEN)
    G, O, D = N_GM_LAYERS, N_NS_LAYERS, D_MODEL
    kv = (NSA_KV_HEADS, HEAD_DIM)
    inp = {}
    inp['x_prompt'] = nrm((BATCH, SEQ, D), 1.0)
    inp['x_sample'] = nrm((DEC_BATCH, DEC_SEQ, D), 1.0)
    inp['state_gla'] = nrm((G, DEC_BATCH, GLA_HEADS, GLA_DK, GLA_DV), 0.3)
    inp['cache_moba_k'] = nrm((G, n_pool, PAGE_SIZE, MOBA_HEADS, HEAD_DIM), 1.0)
    inp['cache_moba_v'] = nrm((G, n_pool, PAGE_SIZE, MOBA_HEADS, HEAD_DIM), 1.0)
    inp['cache_nsa_cmp_k'] = nrm((O, n_pool, PAGE_SIZE) + kv, 1.0)
    inp['cache_nsa_cmp_v'] = nrm((O, n_pool, PAGE_SIZE) + kv, 1.0)
    inp['cache_nsa_slc_k'] = nrm((O, n_pool, PAGE_SIZE) + kv, 1.0)
    inp['cache_nsa_slc_v'] = nrm((O, n_pool, PAGE_SIZE) + kv, 1.0)
    inp['state_nsa_win_k'] = nrm((O, DEC_BATCH, win_rows) + kv, 1.0)
    inp['state_nsa_win_v'] = nrm((O, DEC_BATCH, win_rows) + kv, 1.0)
    perm = jax.random.permutation(next(keys), n_pool)
    inp['page_table'] = perm[:DEC_BATCH * n_pages].reshape(DEC_BATCH, n_pages).astype(jnp.int32)
    inp['gm_norm_mix'] = gain((G, D))
    inp['gm_w_in'] = nrm((G, D, sum(GM_COLS)), D ** -0.5)
    inp['gm_gla_gate_w'] = nrm((G, GLA_GATE_RANK, GLA_HEADS * GLA_DK), GLA_GATE_RANK ** -0.5)
    inp['gm_gla_gate_b'] = nrm((G, GLA_HEADS * GLA_DK), 0.1)
    inp['gm_gla_out_norm'] = gain((G, GLA_HEADS * GLA_DV))
    inp['gm_moba_q_norm'] = gain((G, HEAD_DIM))
    inp['gm_moba_k_norm'] = gain((G, HEAD_DIM))
    inp['gm_w_out'] = nrm((G, GM_MIX, D), 0.5 * GM_MIX ** -0.5)
    inp['gm_norm_ffn'] = gain((G, D))
    inp['gm_ffn_gate'] = nrm((G, D, FFN_DIM), D ** -0.5)
    inp['gm_ffn_up'] = nrm((G, D, FFN_DIM), D ** -0.5)
    inp['gm_ffn_down'] = nrm((G, FFN_DIM, D), 0.5 * FFN_DIM ** -0.5)
    inp['ns_norm_mix'] = gain((O, D))
    inp['ns_w_in'] = nrm((O, D, sum(NS_COLS)), D ** -0.5)
    inp['ns_gate_b'] = nrm((O, NSA_HEADS * 3), 0.1)
    inp['ns_q_norm'] = gain((O, HEAD_DIM))
    inp['ns_kcmp_norm'] = gain((O, HEAD_DIM))
    inp['ns_kslc_norm'] = gain((O, HEAD_DIM))
    inp['ns_kwin_norm'] = gain((O, HEAD_DIM))
    for t in ('k', 'v'):
        inp['ns_cmp' + t + '_pe'] = nrm((O, NSA_CMP_LEN, HEAD_DIM), 0.5)
        inp['ns_cmp' + t + '_w1'] = nrm((O, NSA_CMP_LEN * HEAD_DIM, NSA_CMP_HIDDEN), (NSA_CMP_LEN * HEAD_DIM) ** -0.5)
        inp['ns_cmp' + t + '_b1'] = nrm((O, NSA_CMP_HIDDEN), 0.02)
        inp['ns_cmp' + t + '_w2'] = nrm((O, NSA_CMP_HIDDEN, HEAD_DIM), NSA_CMP_HIDDEN ** -0.5)
        inp['ns_cmp' + t + '_b2'] = nrm((O, HEAD_DIM), 0.02)
    inp['ns_w_out'] = nrm((O, NS_MIX, D), 0.5 * NS_MIX ** -0.5)
    inp['ns_norm_ffn'] = gain((O, D))
    inp['ns_router'] = nrm((O, D, N_EXPERTS), D ** -0.5)
    inp['ns_exp_gate'] = nrm((O, N_EXPERTS, D, EXPERT_DIM), D ** -0.5)
    inp['ns_exp_up'] = nrm((O, N_EXPERTS, D, EXPERT_DIM), D ** -0.5)
    inp['ns_exp_down'] = nrm((O, N_EXPERTS, EXPERT_DIM, D), 0.5 * EXPERT_DIM ** -0.5)
    return inp


def reference(x_prompt, x_sample, state_gla, cache_moba_k, cache_moba_v, cache_nsa_cmp_k, cache_nsa_cmp_v,
              cache_nsa_slc_k, cache_nsa_slc_v, state_nsa_win_k, state_nsa_win_v, page_table,
              gm_norm_mix, gm_w_in, gm_gla_gate_w, gm_gla_gate_b, gm_gla_out_norm, gm_moba_q_norm, gm_moba_k_norm,
              gm_w_out, gm_norm_ffn, gm_ffn_gate, gm_ffn_up, gm_ffn_down,
              ns_norm_mix, ns_w_in, ns_gate_b, ns_q_norm, ns_kcmp_norm, ns_kslc_norm, ns_kwin_norm,
              ns_cmpk_pe, ns_cmpk_w1, ns_cmpk_b1, ns_cmpk_w2, ns_cmpk_b2,
              ns_cmpv_pe, ns_cmpv_w1, ns_cmpv_b1, ns_cmpv_w2, ns_cmpv_b2,
              ns_w_out, ns_norm_ffn, ns_router, ns_exp_gate, ns_exp_up, ns_exp_down):
    past = page_table.shape[1] * cache_moba_k.shape[2]
    xp, xs = x_prompt, x_sample
    gla_p, gla_s, mk_p, mv_p, mk_s, mv_s = [], [], [], [], [], []
    ns_p = [[], [], [], [], [], []]
    ns_s = [[], [], [], [], [], []]
    for layer in range(DEPTH):
        i = layer // 2
        if layer % 2 == 0:
            w = (gm_w_in[i], gm_gla_gate_w[i], gm_gla_gate_b[i], gm_gla_out_norm[i], gm_moba_q_norm[i],
                 gm_moba_k_norm[i], gm_w_out[i])
            s0 = jnp.zeros((xp.shape[0], GLA_HEADS, GLA_DK, GLA_DV), jnp.float32)
            yp, sp, kp, vp = _mix_gm(_rms(xp, gm_norm_mix[i]), 0, s0, _gm_sources_prompt, *w)
            ys, ss, ks, vs = _mix_gm(_rms(xs, gm_norm_mix[i]), past, state_gla[i],
                                     _gm_sources_paged(cache_moba_k[i], cache_moba_v[i], page_table), *w)
            xp, xs = xp + yp, xs + ys
            f = (gm_ffn_gate[i], gm_ffn_up[i], gm_ffn_down[i])
            xp = xp + _swiglu(_rms(xp, gm_norm_ffn[i]), *f)
            xs = xs + _swiglu(_rms(xs, gm_norm_ffn[i]), *f)
            gla_p.append(sp); gla_s.append(ss)
            mk_p.append(kp); mv_p.append(vp); mk_s.append(ks); mv_s.append(vs)
        else:
            w = (ns_w_in[i], ns_gate_b[i], ns_q_norm[i], ns_kcmp_norm[i], ns_kslc_norm[i], ns_kwin_norm[i],
                 (ns_cmpk_pe[i], ns_cmpk_w1[i], ns_cmpk_b1[i], ns_cmpk_w2[i], ns_cmpk_b2[i]),
                 (ns_cmpv_pe[i], ns_cmpv_w1[i], ns_cmpv_b1[i], ns_cmpv_w2[i], ns_cmpv_b2[i]),
                 ns_w_out[i])
            out_p = _mix_ns(_rms(xp, ns_norm_mix[i]), 0, _ns_sources_prompt, *w)
            out_s = _mix_ns(_rms(xs, ns_norm_mix[i]), past,
                            _ns_sources_paged(cache_nsa_cmp_k[i], cache_nsa_cmp_v[i], cache_nsa_slc_k[i],
                                              cache_nsa_slc_v[i], state_nsa_win_k[i], state_nsa_win_v[i], page_table), *w)
            xp, xs = xp + out_p[0], xs + out_s[0]
            m = (ns_router[i], ns_exp_gate[i], ns_exp_up[i], ns_exp_down[i])
            xp = xp + _moe(_rms(xp, ns_norm_ffn[i]), *m)
            xs = xs + _moe(_rms(xs, ns_norm_ffn[i]), *m)
            for lst, a in zip(ns_p, out_p[1:]):
                lst.append(a)
            for lst, a in zip(ns_s, out_s[1:]):
                lst.append(a)
    return (xp, xs,
            jnp.stack(gla_p), jnp.stack(gla_s),
            jnp.stack(mk_p), jnp.stack(mv_p), jnp.stack(mk_s), jnp.stack(mv_s),
            jnp.stack(ns_p[0]), jnp.stack(ns_p[1]), jnp.stack(ns_p[2]), jnp.stack(ns_p[3]),
            jnp.stack(ns_p[4]), jnp.stack(ns_p[5]),
            jnp.stack(ns_s[0]), jnp.stack(ns_s[1]), jnp.stack(ns_s[2]), jnp.stack(ns_s[3]),
            jnp.stack(ns_s[4]), jnp.stack(ns_s[5]))
```

```python
import functools

import numpy as np
import jax
import jax.numpy as jnp
from jax import lax
from jax.experimental import pallas as pl
from jax.experimental.pallas import tpu as pltpu

F32 = jnp.float32
BF16 = jnp.bfloat16

HEAD_DIM = 64
GLA_HEADS, GLA_DK, GLA_DV, GLA_GATE_RANK, GLA_TAU, GLA_CHUNK = 4, 64, 128, 16, 16.0, 16
MOBA_HEADS, MOBA_BLOCK, MOBA_TOPK = 8, 256, 3
NSA_HEADS, NSA_KV_HEADS = 16, 4
NSA_GROUP = NSA_HEADS // NSA_KV_HEADS
NSA_CMP_LEN, NSA_CMP_STRIDE, NSA_CMP_HIDDEN = 32, 16, 64
NSA_SLC_BLOCK, NSA_SLC_TOPN, NSA_INIT_BLOCKS, NSA_LOCAL_BLOCKS, NSA_WINDOW = 64, 16, 1, 2, 512
N_EXPERTS, TOP_K = 8, 2
QUERY_BLOCK = 16
RMS_EPS = 1e-6
NEG_INF = -1e30
FORCE_SCORE = 1e9

LANES = 128
SUBLANES = 8
VMEM_LIMIT_BYTES = 56 * 1024 * 1024

_NT = (((1,), (1,)), ((), ()))
_TN = (((0,), (0,)), ((), ()))
_HI = lax.Precision.HIGHEST


def _cparams(*sem):
    return pltpu.CompilerParams(dimension_semantics=sem, vmem_limit_bytes=VMEM_LIMIT_BYTES)


def _largest_tile(n, cap, mult):
    t = (cap // mult) * mult
    while t > mult and n % t:
        t -= mult
    assert n % t == 0, (n, cap, mult)
    return t


def _alibi_slopes(n):
    return 2.0 ** (-8.0 * jnp.arange(1, n + 1, dtype=F32) / n)


def _silu(x):
    return x * (1.0 / (1.0 + jnp.exp(-x)))


def _loop_tiles(lo, hi, fn, carry, group=2):
    n = hi - lo

    def trip(i, c):
        for u in range(group):
            c = fn(lo + group * i + u, c)
        return c

    carry = lax.fori_loop(0, n // group, trip, carry)
    return lax.fori_loop(hi - n % group, hi, fn, carry)


def _r16(x):
    return x.astype(BF16).astype(F32)


def _proj_t_kernel(x_ref, g_ref, wt_ref, nflag_ref, gain_ref, o_ref, h_scr):
    @pl.when(pl.program_id(1) == 0)
    def _():
        x = x_ref[...]
        ms = jnp.mean(x * x, axis=-1, keepdims=True)
        h_scr[...] = (x * lax.rsqrt(ms + RMS_EPS) * g_ref[...]).astype(BF16)

    y = lax.dot_general(wt_ref[...], h_scr[...], _NT, preferred_element_type=F32)
    tc, tm = y.shape
    y3 = y.reshape(tc // HEAD_DIM, HEAD_DIM, tm)
    ms = jnp.mean(y3 * y3, axis=1, keepdims=True)
    nf = nflag_ref[...].reshape(tc // HEAD_DIM, HEAD_DIM, 1)
    gn = gain_ref[...].reshape(tc // HEAD_DIM, HEAD_DIM, 1)
    scale = jnp.where(nf > 0.0, lax.rsqrt(ms + RMS_EPS), 1.0) * gn
    o_ref[...] = (y3 * scale).reshape(tc, tm)


def _proj_t(x, row0, bsz, seq, g, wt, nflag, gain, tm):
    d = x.shape[1]
    c = wt.shape[0]
    tc = _largest_tile(c, 640, LANES)
    nt = seq // tm
    return pl.pallas_call(
        _proj_t_kernel,
        grid=(bsz * nt, c // tc),
        in_specs=[
            pl.BlockSpec((tm, d), lambda i, j: (row0 // tm + i, 0)),
            pl.BlockSpec((1, d), lambda i, j: (0, 0)),
            pl.BlockSpec((tc, d), lambda i, j: (j, 0)),
            pl.BlockSpec((tc, 1), lambda i, j: (j, 0)),
            pl.BlockSpec((tc, 1), lambda i, j: (j, 0)),
        ],
        out_specs=pl.BlockSpec((None, tc, tm), lambda i, j: (i // nt, j, i % nt)),
        out_shape=jax.ShapeDtypeStruct((bsz, c, seq), F32),
        scratch_shapes=[pltpu.VMEM((tm, d), BF16)],
        compiler_params=_cparams("parallel", "arbitrary"),
        name="proj_t",
    )(x, g, wt, nflag, gain)


def _outproj_t_kernel(*refs, n_in):
    x_ref, a_refs, w_ref, o_ref = refs[0], refs[1:1 + n_in], refs[1 + n_in], refs[2 + n_in]
    a = jnp.concatenate([r[...].astype(BF16) for r in a_refs], axis=0)
    o_ref[...] = x_ref[...] + lax.dot_general(a, w_ref[...], _TN, preferred_element_type=F32)


def _outproj_t(x, row0, a_list, w, tm):
    d = x.shape[1]
    bsz, _, seq = a_list[0].shape
    nt = seq // tm
    a_specs = [pl.BlockSpec((None, a.shape[1], tm), lambda i: (i // nt, 0, i % nt)) for a in a_list]
    return pl.pallas_call(
        functools.partial(_outproj_t_kernel, n_in=len(a_list)),
        grid=(bsz * nt,),
        in_specs=[pl.BlockSpec((tm, d), lambda i: (row0 // tm + i, 0))] + a_specs
        + [pl.BlockSpec(w.shape, lambda i: (0, 0))],
        out_specs=pl.BlockSpec((tm, d), lambda i: (i, 0)),
        out_shape=jax.ShapeDtypeStruct((bsz * seq, d), F32),
        compiler_params=_cparams("parallel"),
        name="outproj_t",
    )(x, *a_list, w)


def _outproj_r_kernel(x_ref, a_ref, w_ref, o_ref):
    o_ref[...] = x_ref[...] + jnp.dot(a_ref[...].astype(BF16), w_ref[...], preferred_element_type=F32)


def _outproj_r(x, a, w):
    return pl.pallas_call(
        _outproj_r_kernel,
        out_shape=jax.ShapeDtypeStruct(x.shape, F32),
        compiler_params=pltpu.CompilerParams(vmem_limit_bytes=VMEM_LIMIT_BYTES),
        name="outproj_r",
    )(x, a, w)


def _ffn_kernel(x_ref, g_ref, wg_ref, wu_ref, wd_ref, o_ref, h_scr, acc_scr):
    j = pl.program_id(1)

    @pl.when(j == 0)
    def _():
        x = x_ref[...]
        ms = jnp.mean(x * x, axis=-1, keepdims=True)
        h_scr[...] = (x * lax.rsqrt(ms + RMS_EPS) * g_ref[...]).astype(BF16)
        acc_scr[...] = jnp.zeros_like(acc_scr)

    h = h_scr[...]
    gate = jnp.dot(h, wg_ref[...], preferred_element_type=F32)
    up = jnp.dot(h, wu_ref[...], preferred_element_type=F32)
    act = (_silu(gate) * up).astype(BF16)
    acc_scr[...] += jnp.dot(act, wd_ref[...], preferred_element_type=F32)

    @pl.when(j == pl.num_programs(1) - 1)
    def _():
        o_ref[...] = x_ref[...] + acc_scr[...]


def _ffn(x, g, wg, wu, wd, tm):
    n, d = x.shape
    f = wg.shape[1]
    tf = _largest_tile(f, 1536, LANES)
    return pl.pallas_call(
        _ffn_kernel,
        grid=(n // tm, f // tf),
        in_specs=[
            pl.BlockSpec((tm, d), lambda i, j: (i, 0)),
            pl.BlockSpec((1, d), lambda i, j: (0, 0)),
            pl.BlockSpec((d, tf), lambda i, j: (0, j)),
            pl.BlockSpec((d, tf), lambda i, j: (0, j)),
            pl.BlockSpec((tf, d), lambda i, j: (j, 0)),
        ],
        out_specs=pl.BlockSpec((tm, d), lambda i, j: (i, 0)),
        out_shape=jax.ShapeDtypeStruct((n, d), F32),
        scratch_shapes=[pltpu.VMEM((tm, d), BF16), pltpu.VMEM((tm, d), F32)],
        compiler_params=_cparams("parallel", "arbitrary"),
        name="ffn",
    )(x, g, wg, wu, wd)


def _moba_p_kernel(slopes_ref, q_ref, k_ref, v_ref, o_ref, sel_scr, *, n_blk):
    h = pl.program_id(1)
    qi = pl.program_id(2)
    blk = MOBA_BLOCK
    slope = slopes_ref[h]
    q = q_ref[...]
    qb = q.astype(BF16)

    gates = []
    for j in range(n_blk):
        kmean = jnp.mean(k_ref[:, j * blk:(j + 1) * blk], axis=1, keepdims=True)
        gates.append(jnp.sum(qb.astype(F32) * _r16(kmean), axis=0, keepdims=True))
    for j in range(n_blk):
        cnt = jnp.zeros_like(gates[0])
        for j2 in range(n_blk):
            if j2 == j:
                continue
            beats = (gates[j2] >= gates[j]) if j2 < j else (gates[j2] > gates[j])
            cnt = cnt + jnp.where(beats, 1.0, 0.0) * (j2 < qi).astype(F32)
        sel_scr[j:j + 1, :] = jnp.where(cnt < float(MOBA_TOPK), 0.0, -NEG_INF)

    sub = lax.broadcasted_iota(jnp.int32, (blk, blk), 0)
    lane = lax.broadcasted_iota(jnp.int32, (blk, blk), 1)
    rel = (lane - sub).astype(F32)
    alibi = slope * rel

    def attend(j, penalty, row_const, carry):
        m, l, acc = carry
        start = pl.multiple_of(j * blk, blk)
        kj = k_ref[:, pl.ds(start, blk)].astype(BF16)
        vj = v_ref[:, pl.ds(start, blk)].astype(BF16)
        s = lax.dot_general(kj, qb, _TN, preferred_element_type=F32) - penalty
        m_new = jnp.maximum(m, jnp.max(s, axis=0, keepdims=True) - row_const)
        p = jnp.exp(s - (m_new + row_const))
        alpha = jnp.exp(m - m_new)
        l = alpha * l + jnp.sum(p, axis=0, keepdims=True)
        acc = alpha * acc + jnp.dot(vj, p.astype(BF16), preferred_element_type=F32)
        return m_new, l, acc

    init = (jnp.full((1, blk), NEG_INF, F32), jnp.zeros((1, blk), F32), jnp.zeros((HEAD_DIM, blk), F32))
    carry = attend(qi, jnp.where(rel >= 0.0, alibi, -NEG_INF), 0.0, init)

    def past(j, carry):
        row_const = slope * ((qi - j) * blk).astype(F32) + sel_scr[pl.ds(j, 1), :]
        return attend(j, alibi, row_const, carry)

    m, l, acc = _loop_tiles(0, qi, past, carry)
    o_ref[...] = acc / l


def _moba_prompt(yt, slopes, q_row0, k_row0, v_row0):
    bsz, _, seq = yt.shape
    assert seq % MOBA_BLOCK == 0
    n_blk = seq // MOBA_BLOCK
    hd = HEAD_DIM

    def kvspec(row0):
        return pl.BlockSpec((None, hd, seq), lambda b, h, i: (b, row0 // hd + h, 0))

    return pl.pallas_call(
        functools.partial(_moba_p_kernel, n_blk=n_blk),
        grid=(bsz, MOBA_HEADS, n_blk),
        in_specs=[
            pl.BlockSpec(memory_space=pltpu.SMEM),
            pl.BlockSpec((None, hd, MOBA_BLOCK), lambda b, h, i: (b, q_row0 // hd + h, i)),
            kvspec(k_row0),
            kvspec(v_row0),
        ],
        out_specs=pl.BlockSpec((None, hd, MOBA_BLOCK), lambda b, h, i: (b, h, i)),
        out_shape=jax.ShapeDtypeStruct((bsz, MOBA_HEADS * hd, seq), F32),
        scratch_shapes=[pltpu.VMEM((max(n_blk, SUBLANES), MOBA_BLOCK), F32)],
        compiler_params=_cparams("parallel", "parallel", "arbitrary"),
        name="moba_prompt",
    )(slopes, yt, yt, yt)


def _log_sigmoid(x):
    return jnp.minimum(x, 0.0) - jnp.log1p(jnp.exp(-jnp.abs(x)))


def _gla_p_kernel(q_ref, k_ref, v_ref, ga_ref, rg_ref, gw_ref, gb_ref, on_ref, o_ref, s_ref, st_scr):
    ti = pl.program_id(2)
    ch = GLA_CHUNK
    n_ch = LANES // ch

    @pl.when(ti == 0)
    def _():
        st_scr[...] = jnp.zeros_like(st_scr)

    q = q_ref[...]
    k = k_ref[...]
    v = v_ref[...]
    ga = ga_ref[0:GLA_GATE_RANK, :]
    x = jnp.dot(gw_ref[...].astype(BF16), ga.astype(BF16), preferred_element_type=F32) + gb_ref[...]
    g = _log_sigmoid(x) / GLA_TAU

    lane_k = lax.broadcasted_iota(jnp.int32, (GLA_DK, LANES), 1)
    pos = lane_k % ch
    b = g
    c = g
    for sh in (1, 2, 4, 8):
        b = b + jnp.where(pos >= sh, pltpu.roll(b, sh, 1), 0.0)
        c = c + jnp.where(pos < ch - sh, pltpu.roll(c, LANES - sh, 1), 0.0)
    tot = b + c - g

    qd = q * jnp.exp(b)
    kd = k * jnp.exp(tot - b)

    o = jnp.zeros((GLA_DV, LANES), F32)
    for dl in range(ch):
        if dl == 0:
            kk, bb, vv = k, b, v
        else:
            kk, bb, vv = pltpu.roll(k, dl, 1), pltpu.roll(b, dl, 1), pltpu.roll(v, dl, 1)
        a = jnp.where(pos >= dl, q * kk * jnp.exp(b - bb), 0.0)
        o = o + jnp.sum(a, axis=0, keepdims=True) * vv

    cid_v = lax.broadcasted_iota(jnp.int32, (GLA_DV, LANES), 1) // ch
    cid_k = lane_k // ch
    vm = jnp.concatenate([jnp.where(cid_v == cc, v, 0.0) for cc in range(n_ch)], axis=0).astype(BF16)
    ut = lax.dot_general(vm, kd.astype(BF16), _NT, preferred_element_type=F32)
    dt = jnp.exp(tot).T
    st = st_scr[...]
    pieces = []
    for cc in range(n_ch):
        pieces.append(st.astype(BF16))
        st = st * dt[cc * ch:cc * ch + 1, :] + ut[cc * GLA_DV:(cc + 1) * GLA_DV, :]
    st_scr[...] = st
    sstack = jnp.concatenate(pieces, axis=1)
    qm = jnp.concatenate([jnp.where(cid_k == cc, qd, 0.0) for cc in range(n_ch)], axis=0).astype(BF16)
    o = o + jnp.dot(sstack, qm, preferred_element_type=F32)

    ms = jnp.mean(o * o, axis=0, keepdims=True)
    o_ref[...] = (o * lax.rsqrt(ms + RMS_EPS) * on_ref[...]) * _silu(rg_ref[...])

    @pl.when(ti == pl.num_programs(2) - 1)
    def _():
        s_ref[...] = st.T


def _gla_prompt(yt, rows, gw_t, gb, on):
    bsz, _, seq = yt.shape
    nt = seq // LANES
    q0, k0, v0, rg0, ga0 = rows

    def at(row0, size):
        return pl.BlockSpec((None, size, LANES), lambda b, h, t: (b, row0 // size + h, t))

    hspec = lambda size, last: pl.BlockSpec((None, size, last), lambda b, h, t: (h, 0, 0))
    return pl.pallas_call(
        _gla_p_kernel,
        grid=(bsz, GLA_HEADS, nt),
        in_specs=[
            at(q0, GLA_DK), at(k0, GLA_DK), at(v0, GLA_DV),
            pl.BlockSpec((None, LANES, LANES), lambda b, h, t: (b, ga0 // LANES, t)),
            at(rg0, GLA_DV),
            hspec(GLA_DK, GLA_GATE_RANK), hspec(GLA_DK, 1), hspec(GLA_DV, 1),
        ],
        out_specs=[
            pl.BlockSpec((None, GLA_DV, LANES), lambda b, h, t: (b, h, t)),
            pl.BlockSpec((None, None, GLA_DK, GLA_DV), lambda b, h, t: (b, h, 0, 0)),
        ],
        out_shape=[
            jax.ShapeDtypeStruct((bsz, GLA_HEADS * GLA_DV, seq), F32),
            jax.ShapeDtypeStruct((bsz, GLA_HEADS, GLA_DK, GLA_DV), F32),
        ],
        scratch_shapes=[pltpu.VMEM((GLA_DV, GLA_DK), F32)],
        compiler_params=_cparams("parallel", "parallel", "arbitrary"),
        name="gla_prompt",
    )(yt, yt, yt, yt, yt, gw_t, gb, on)


def _gla_s_kernel(q_ref, k_ref, v_ref, rg_ref, ga_ref, gw_ref, gb_ref, on_ref, s0_ref, o_ref, s_ref):
    x = jnp.sum(gw_ref[...] * ga_ref[...], axis=-1, keepdims=True) + gb_ref[...]
    a = jnp.exp(_log_sigmoid(x) / GLA_TAU)
    s = a * s0_ref[...] + k_ref[...] * v_ref[...]
    s_ref[...] = s
    o = jnp.sum(q_ref[...] * s, axis=1, keepdims=True)
    ms = jnp.mean(o * o, axis=-1, keepdims=True)
    o_ref[...] = (o * lax.rsqrt(ms + RMS_EPS) * on_ref[...]) * _silu(rg_ref[...])


def _gla_sample(q, k, v, rg, ga, gw_t, gb, on, s0):
    bsz = q.shape[0]
    per_b = lambda a: pl.BlockSpec((None,) + a.shape[1:], lambda b: (b, 0, 0, 0))
    full = lambda a: pl.BlockSpec(a.shape, lambda b: (0,) * a.ndim)
    return pl.pallas_call(
        _gla_s_kernel,
        grid=(bsz,),
        in_specs=[per_b(q), per_b(k), per_b(v), per_b(rg), per_b(ga), full(gw_t), full(gb), full(on), per_b(s0)],
        out_specs=[per_b(v), per_b(s0)],
        out_shape=[jax.ShapeDtypeStruct(v.shape, F32), jax.ShapeDtypeStruct(s0.shape, F32)],
        compiler_params=_cparams("parallel"),
        name="gla_sample",
    )(q, k, v, rg, ga, gw_t, gb, on, s0)


MOBA_PAGES_PER_STEP = 8


def _moba_gate_kernel(pt_ref, q_ref, *refs, n_blk):
    pages, (gate_ref, top_ref) = refs[:MOBA_PAGES_PER_STEP], refs[MOBA_PAGES_PER_STEP:]
    s = pl.program_id(1)
    per_step = MOBA_PAGES_PER_STEP // 2
    lane = lax.broadcasted_iota(jnp.int32, (MOBA_HEADS, LANES), 1)

    @pl.when(s == 0)
    def _():
        gate_ref[...] = jnp.full_like(gate_ref, NEG_INF)

    q = _r16(q_ref[...])
    gate = gate_ref[...]
    for u in range(per_step):
        ksum = pages[2 * u][...] + pages[2 * u + 1][...]
        kmean = jnp.sum(ksum, axis=-1, keepdims=True) / float(MOBA_BLOCK)
        gv = jnp.sum(q * _r16(kmean), axis=1)
        gate = jnp.where(lane == s * per_step + u, gv, gate)
    gate_ref[...] = gate

    @pl.when(s == pl.num_programs(1) - 1)
    def _():
        g = gate
        top = jnp.zeros((MOBA_HEADS, LANES), jnp.int32)
        for r in range(MOBA_TOPK):
            m = jnp.max(g, axis=-1, keepdims=True)
            idx = jnp.min(jnp.where(g == m, lane, LANES), axis=-1, keepdims=True)
            top = jnp.where(lane == r, idx, top)
            g = jnp.where(lane == idx, -jnp.inf, g)
        top_ref[...] = top


def _moba_gates(page_table, q, kpool_t):
    bsz, n_pages = page_table.shape
    n_blk = n_pages // 2
    assert MOBA_TOPK <= n_blk <= LANES and n_pages % MOBA_PAGES_PER_STEP == 0
    pps = MOBA_PAGES_PER_STEP

    def page_spec(u):
        return pl.BlockSpec((None, MOBA_HEADS, HEAD_DIM, LANES), lambda b, s, pt: (pt[b, s * pps + u], 0, 0, 0))

    out_spec = pl.BlockSpec((None, MOBA_HEADS, LANES), lambda b, s, pt: (b, 0, 0))
    return pl.pallas_call(
        functools.partial(_moba_gate_kernel, n_blk=n_blk),
        grid_spec=pltpu.PrefetchScalarGridSpec(
            num_scalar_prefetch=1,
            grid=(bsz, n_pages // pps),
            in_specs=[pl.BlockSpec((None, MOBA_HEADS, HEAD_DIM, 1), lambda b, s, pt: (b, 0, 0, 0))]
            + [page_spec(u) for u in range(pps)],
            out_specs=[out_spec, out_spec],
        ),
        out_shape=[jax.ShapeDtypeStruct((bsz, MOBA_HEADS, LANES), F32),
                   jax.ShapeDtypeStruct((bsz, MOBA_HEADS, LANES), jnp.int32)],
        compiler_params=_cparams("parallel", "arbitrary"),
        name="moba_gates",
    )(page_table, q, *([kpool_t] * pps))


def _moba_s_kernel(pt_ref, top_ref, slopes_ref, q_ref, kn_ref, vn_ref, *refs, past):
    n_pg = 2 * MOBA_TOPK
    k_refs, v_refs, o_ref = refs[:n_pg], refs[n_pg:2 * n_pg], refs[2 * n_pg]
    b = pl.program_id(0)
    h = pl.program_id(1)
    slope = slopes_ref[h]
    q = _r16(q_ref[...])
    lane = lax.broadcasted_iota(jnp.int32, (1, LANES), 1)
    scores = []
    for r in range(MOBA_TOPK):
        for half in range(2):
            kpos = top_ref[b, h * MOBA_TOPK + r] * MOBA_BLOCK + half * LANES + lane
            s = jnp.sum(_r16(k_refs[2 * r + half][...]) * q, axis=0, keepdims=True)
            scores.append(s - slope * (past - kpos).astype(F32))
    s_new = jnp.sum(_r16(kn_ref[...]) * q, axis=0, keepdims=True)
    m = s_new
    for s in scores:
        m = jnp.maximum(m, jnp.max(s, axis=-1, keepdims=True))
    p_new = jnp.exp(s_new - m)
    probs = [jnp.exp(s - m) for s in scores]
    l = p_new
    for p in probs:
        l = l + jnp.sum(p, axis=-1, keepdims=True)
    o = _r16(p_new / l) * _r16(vn_ref[...])
    for p, v_ref in zip(probs, v_refs):
        o = o + jnp.sum(_r16(v_ref[...]) * _r16(p / l), axis=-1, keepdims=True)
    o_ref[...] = o


def _moba_sample(page_table, top, slopes, q, k_new, v_new, kpool_t, vpool_t):
    bsz, n_pages = page_table.shape
    past = n_pages * LANES

    def col_spec():
        return pl.BlockSpec((None, None, HEAD_DIM, 1), lambda b, h, pt, tp: (b, h, 0, 0))

    def page_spec(r, half):
        return pl.BlockSpec((None, None, HEAD_DIM, LANES),
                            lambda b, h, pt, tp: (pt[b, 2 * tp[b, h * MOBA_TOPK + r] + half], h, 0, 0))

    pages = [page_spec(r, half) for r in range(MOBA_TOPK) for half in range(2)]
    return pl.pallas_call(
        functools.partial(_moba_s_kernel, past=past),
        grid_spec=pltpu.PrefetchScalarGridSpec(
            num_scalar_prefetch=2,
            grid=(bsz, MOBA_HEADS),
            in_specs=[pl.BlockSpec(memory_space=pltpu.SMEM), col_spec(), col_spec(), col_spec()] + pages + pages,
            out_specs=col_spec(),
        ),
        out_shape=jax.ShapeDtypeStruct((bsz, MOBA_HEADS, HEAD_DIM, 1), F32),
        compiler_params=_cparams("parallel", "parallel"),
        name="moba_sample",
    )(page_table, top, slopes, q, k_new, v_new, *([kpool_t] * len(pages)), *([vpool_t] * len(pages)))


def _cmp_stage1_kernel(*refs, n_in):
    x_refs, w_ref, o_ref, x_scr = refs[:n_in], refs[n_in], refs[n_in + 1], refs[n_in + 2]
    st = NSA_CMP_STRIDE
    row = 0
    for r in x_refs:
        w = r.shape[-1]
        x_scr[row:row + w, :] = r[...].T
        row += w
    n_chunk = row // st
    xcat = jnp.concatenate([x_scr[pl.ds(s, n_chunk, stride=st), :] for s in range(st)], axis=1)
    o_ref[...] = jnp.dot(xcat.astype(BF16), w_ref[...], preferred_element_type=F32)


def _cmp_w1cat(w1):
    ratio = NSA_CMP_LEN // NSA_CMP_STRIDE
    w = w1.reshape(ratio, NSA_CMP_STRIDE * HEAD_DIM, NSA_CMP_HIDDEN)
    return jnp.concatenate([w[r] for r in range(ratio)], axis=1).astype(BF16)


def _cmp_stage1_dense(xt, row0, w1cat):
    bsz, _, seq = xt.shape
    n_chunk = seq // NSA_CMP_STRIDE
    return pl.pallas_call(
        functools.partial(_cmp_stage1_kernel, n_in=1),
        grid=(bsz, NSA_KV_HEADS),
        in_specs=[pl.BlockSpec((None, HEAD_DIM, seq), lambda b, g: (b, row0 // HEAD_DIM + g, 0)),
                  pl.BlockSpec(w1cat.shape, lambda b, g: (0, 0))],
        out_specs=pl.BlockSpec((None, None, n_chunk, LANES), lambda b, g: (b, g, 0, 0)),
        out_shape=jax.ShapeDtypeStruct((bsz, NSA_KV_HEADS, n_chunk, LANES), F32),
        scratch_shapes=[pltpu.VMEM((seq, HEAD_DIM), F32)],
        compiler_params=_cparams("parallel", "parallel"),
        name="cmp_stage1_dense",
    )(xt, w1cat)


CMP_PAGES_PER_STEP = 16


def _cmp_stage1_paged_kernel(pt_ref, *refs):
    pps, ng, st = CMP_PAGES_PER_STEP, NSA_KV_HEADS, NSA_CMP_STRIDE
    x_refs, w_ref, o_ref, x_scr = refs[:pps], refs[pps], refs[pps + 1], refs[pps + 2]
    for g in range(ng):
        for u, r in enumerate(x_refs):
            row = (g * pps + u) * LANES
            x_scr[row:row + LANES, :] = r[g].T
    n_chunk = ng * pps * LANES // st
    xcat = jnp.concatenate([x_scr[pl.ds(s, n_chunk, stride=st), :] for s in range(st)], axis=1)
    y = jnp.dot(xcat.astype(BF16), w_ref[...], preferred_element_type=F32)
    o_ref[...] = y.reshape(ng, n_chunk // ng, LANES)


def _cmp_stage1_paged(page_table, pool_t, w1cat):
    bsz, n_pages = page_table.shape
    pps = CMP_PAGES_PER_STEP
    assert n_pages % pps == 0
    cps = pps * LANES // NSA_CMP_STRIDE
    n_chunk = n_pages * LANES // NSA_CMP_STRIDE

    def page_spec(u):
        return pl.BlockSpec((None, NSA_KV_HEADS, HEAD_DIM, LANES), lambda b, s, pt: (pt[b, s * pps + u], 0, 0, 0))

    return pl.pallas_call(
        _cmp_stage1_paged_kernel,
        grid_spec=pltpu.PrefetchScalarGridSpec(
            num_scalar_prefetch=1,
            grid=(bsz, n_pages // pps),
            in_specs=[page_spec(u) for u in range(pps)] + [pl.BlockSpec(w1cat.shape, lambda b, s, pt: (0, 0))],
            out_specs=pl.BlockSpec((None, NSA_KV_HEADS, cps, LANES), lambda b, s, pt: (b, 0, s, 0)),
            scratch_shapes=[pltpu.VMEM((NSA_KV_HEADS * pps * LANES, HEAD_DIM), F32)],
        ),
        out_shape=jax.ShapeDtypeStruct((bsz, NSA_KV_HEADS, n_chunk, LANES), F32),
        compiler_params=_cparams("parallel", "arbitrary"),
        name="cmp_stage1_paged",
    )(page_table, *([pool_t] * pps), w1cat)


def _cmp_stage2_kernel(p_ref, pe_ref, w1_ref, b1_ref, w2_ref, b2_ref, gain_ref, o_ref, *, normalize):
    p = p_ref[...]
    n = p.shape[0]
    const = jnp.dot(pe_ref[...].astype(BF16), w1_ref[...].astype(BF16), preferred_element_type=F32) + b1_ref[...]
    shifted = pltpu.roll(pltpu.roll(p, n - 1, 0), NSA_CMP_HIDDEN, 1)
    hid = (p + shifted)[:, :NSA_CMP_HIDDEN] + const
    out = jnp.dot(_silu(hid).astype(BF16), w2_ref[...], preferred_element_type=F32) + b2_ref[...]
    if normalize:
        ms = jnp.mean(out * out, axis=-1, keepdims=True)
        out = out * lax.rsqrt(ms + RMS_EPS) * gain_ref[...]
    o_ref[...] = out.T


def _cmp_stage2(parts, pe, w1, b1, w2, b2, gain):
    bsz, ng, n_chunk, _ = parts.shape
    normalize = gain is not None
    gain = jnp.ones((HEAD_DIM,), F32) if gain is None else gain
    full = lambda a: pl.BlockSpec(a.shape, lambda b, g: (0,) * a.ndim)
    args = (pe.reshape(1, -1), w1, b1.reshape(1, -1), w2.astype(BF16), b2.reshape(1, -1), gain.reshape(1, -1))
    return pl.pallas_call(
        functools.partial(_cmp_stage2_kernel, normalize=normalize),
        grid=(bsz, ng),
        in_specs=[pl.BlockSpec((None, None, n_chunk, LANES), lambda b, g: (b, g, 0, 0))] + [full(a) for a in args],
        out_specs=pl.BlockSpec((None, None, HEAD_DIM, n_chunk), lambda b, g: (b, g, 0, 0)),
        out_shape=jax.ShapeDtypeStruct((bsz, ng, HEAD_DIM, n_chunk), F32),
        compiler_params=_cparams("parallel", "parallel"),
        name="cmp_stage2",
    )(parts, *args)


NSA_TQ = 128


def _nsa_p_kernel(slopes_ref, q_ref, kc_ref, vc_ref, ks_ref, vs_ref, kw_ref, vw_ref, gt_ref, gb_ref, ov_ref,
                  o_ref, sel_scr, *, n_cmp, n_blk):
    g = pl.program_id(1)
    qi = pl.program_id(2)
    tq, grp, hd = NSA_TQ, NSA_GROUP, HEAD_DIM
    wq = grp * tq
    q4 = jnp.concatenate([q_ref[r * hd:(r + 1) * hd, :] for r in range(grp)], axis=1)
    qb = q4.astype(BF16)
    lane = lax.broadcasted_iota(jnp.int32, (1, wq), 1)
    qpos = qi * tq + lane % tq
    slope = jnp.zeros((1, wq), F32)
    for r in range(grp):
        slope = jnp.where(lane // tq == r, slopes_ref[g * grp + r], slope)

    def softmax_block(kt, vt, kpos, mask, carry):
        m, l, acc = carry
        s = lax.dot_general(kt, qb, _TN, preferred_element_type=F32)
        s = jnp.where(mask, s - slope * (qpos - kpos).astype(F32), NEG_INF)
        m_new = jnp.maximum(m, jnp.max(s, axis=0, keepdims=True))
        p = jnp.where(mask, jnp.exp(s - m_new), 0.0)
        alpha = jnp.exp(m - m_new)
        l = alpha * l + jnp.sum(p, axis=0, keepdims=True)
        acc = alpha * acc + jnp.dot(vt, p.astype(BF16), preferred_element_type=F32)
        return (m_new, l, acc), p

    def finish(carry):
        _, l, acc = carry
        return acc / jnp.where(l > 0.0, l, 1.0)

    init = (jnp.full((1, wq), NEG_INF, F32), jnp.zeros((1, wq), F32), jnp.zeros((hd, wq), F32))

    nc = kc_ref.shape[-1]
    ci = lax.broadcasted_iota(jnp.int32, (nc, 1), 0)
    c_end = ci * NSA_CMP_STRIDE + (NSA_CMP_LEN - 1)
    mask_c = (c_end <= qpos) & (ci < n_cmp)
    carry_c, p_c = softmax_block(kc_ref[...].astype(BF16), vc_ref[...].astype(BF16), c_end, mask_c, init)
    o_c = finish(carry_c)
    prob_c = p_c / jnp.where(carry_c[1] > 0.0, carry_c[1], 1.0)

    imp4 = jnp.dot(ov_ref[...].astype(BF16), prob_c.astype(BF16), preferred_element_type=F32)
    imp = imp4[:, 0:tq]
    for r in range(1, grp):
        imp = imp + imp4[:, r * tq:(r + 1) * tq]
    bi = lax.broadcasted_iota(jnp.int32, (n_blk, tq), 0)
    cur = (qi * tq + lax.broadcasted_iota(jnp.int32, (n_blk, tq), 1)) // NSA_SLC_BLOCK
    allowed = bi <= cur
    forced = (bi < NSA_INIT_BLOCKS) | (cur - bi < NSA_LOCAL_BLOCKS)
    val = jnp.where(allowed & forced, FORCE_SCORE, jnp.where(allowed, imp, NEG_INF))
    rank = jnp.zeros((n_blk, tq), F32)
    for j in range(n_blk):
        vj = val[j:j + 1, :]
        beats = (vj > val) | ((vj == val) & (j < bi))
        rank = rank + jnp.where(beats, 1.0, 0.0)
    big = -NEG_INF
    pen = jnp.where((rank < float(NSA_SLC_TOPN)) & allowed, 0.0, big)
    sel_scr[0:n_blk, :] = jnp.concatenate([pen] * grp, axis=1)

    half = NSA_SLC_BLOCK
    ksub = lax.broadcasted_iota(jnp.int32, (tq, 1), 0)
    rel = (lane % tq - ksub).astype(F32)
    alibi = slope * rel
    pen_diag = jnp.where(rel >= 0.0, alibi, big)
    pen_low = jnp.where(rel <= 0.0, alibi, big)

    def tile_update(carry, k_ref_, v_ref_, jj, penalty, parts):
        m, l, acc = carry
        start = pl.multiple_of(jj * tq, tq)
        kt = k_ref_[:, pl.ds(start, tq)].astype(BF16)
        vt = v_ref_[:, pl.ds(start, tq)].astype(BF16)
        s = lax.dot_general(kt, qb, _TN, preferred_element_type=F32) - penalty
        m_new = m
        for r0, r1, c in parts:
            m_new = jnp.maximum(m_new, jnp.max(s[r0:r1], axis=0, keepdims=True) - c)
        alpha = jnp.exp(m - m_new)
        l = alpha * l
        ps = []
        for r0, r1, c in parts:
            p = jnp.exp(s[r0:r1] - (m_new + c))
            l = l + jnp.sum(p, axis=0, keepdims=True)
            ps.append(p.astype(BF16))
        pb = ps[0] if len(ps) == 1 else jnp.concatenate(ps, axis=0)
        acc = alpha * acc + jnp.dot(vt, pb, preferred_element_type=F32)
        return m_new, l, acc

    def slc_parts(jj, row_const):
        return [(0, half, row_const + sel_scr[pl.ds(2 * jj, 1), :]),
                (half, tq, row_const + sel_scr[pl.ds(2 * jj + 1, 1), :])]

    def tile_const(jj):
        return slope * ((qi - jj) * tq).astype(F32)

    carry = tile_update(init, ks_ref, vs_ref, qi, pen_diag, slc_parts(qi, 0.0))
    carry = _loop_tiles(
        0, qi, lambda jj, c: tile_update(c, ks_ref, vs_ref, jj, alibi, slc_parts(jj, tile_const(jj))), carry)
    o_s = finish(carry)

    n_wt = NSA_WINDOW // tq
    carry = tile_update(init, kw_ref, vw_ref, qi, pen_diag, [(0, tq, 0.0)])
    carry = _loop_tiles(
        jnp.maximum(qi - (n_wt - 1), 0), qi,
        lambda jj, c: tile_update(c, kw_ref, vw_ref, jj, alibi, [(0, tq, tile_const(jj))]), carry)
    carry = lax.fori_loop(
        0, jnp.where(qi >= n_wt, 1, 0),
        lambda _, c: tile_update(c, kw_ref, vw_ref, qi - n_wt, pen_low, [(0, tq, tile_const(qi - n_wt))]), carry)
    o_w = finish(carry)

    outs = []
    for r in range(grp):
        acc = jnp.zeros((hd, tq), F32)
        for br, o_b in enumerate((o_c, o_s, o_w)):
            row = (g * grp + r) * 3 + br
            gt = gt_ref[pl.ds(row, 1), :] + gb_ref[pl.ds(row, 1), :]
            acc = acc + (1.0 / (1.0 + jnp.exp(-gt))) * o_b[:, r * tq:(r + 1) * tq]
        outs.append(acc)
    o_ref[...] = jnp.concatenate(outs, axis=0)


def _nsa_overlap_t(n_cmp_pad, n_blk):
    c_start = np.arange(n_cmp_pad) * NSA_CMP_STRIDE
    c_end = c_start + NSA_CMP_LEN - 1
    b_start = np.arange(n_blk) * NSA_SLC_BLOCK
    ov = (c_start[None, :] < b_start[:, None] + NSA_SLC_BLOCK) & (c_end[None, :] >= b_start[:, None])
    return jnp.asarray(ov.astype(np.float32))


def _nsa_prompt(yt, kcmp_t, vcmp_t, slopes, rows, gate_b_col):
    bsz, _, seq = yt.shape
    q0, ks0, vs0, kw0, vw0, g0 = rows
    n_chunk = kcmp_t.shape[-1]
    n_cmp = max((seq - NSA_CMP_LEN) // NSA_CMP_STRIDE + 1, 1)
    n_blk = max(-(-seq // NSA_SLC_BLOCK), NSA_SLC_TOPN)
    assert seq % NSA_TQ == 0 and n_blk * NSA_SLC_BLOCK == seq and n_blk % SUBLANES == 0
    assert NSA_TQ == 2 * NSA_SLC_BLOCK and NSA_WINDOW % NSA_TQ == 0 and NSA_LOCAL_BLOCKS >= 1
    hd, wq = HEAD_DIM, NSA_GROUP * NSA_TQ
    ov = _nsa_overlap_t(n_chunk, n_blk)

    def kv(row0):
        return pl.BlockSpec((None, hd, seq), lambda b, g, i: (b, row0 // hd + g, 0))

    cmp_spec = pl.BlockSpec((None, None, hd, n_chunk), lambda b, g, i: (b, g, 0, 0))
    return pl.pallas_call(
        functools.partial(_nsa_p_kernel, n_cmp=n_cmp, n_blk=n_blk),
        grid=(bsz, NSA_KV_HEADS, seq // NSA_TQ),
        in_specs=[
            pl.BlockSpec(memory_space=pltpu.SMEM),
            pl.BlockSpec((None, NSA_GROUP * hd, NSA_TQ), lambda b, g, i: (b, q0 // (NSA_GROUP * hd) + g, i)),
            cmp_spec, cmp_spec, kv(ks0), kv(vs0), kv(kw0), kv(vw0),
            pl.BlockSpec((None, LANES, NSA_TQ), lambda b, g, i: (b, g0 // LANES, i)),
            pl.BlockSpec((LANES, 1), lambda b, g, i: (0, 0)),
            pl.BlockSpec(ov.shape, lambda b, g, i: (0, 0)),
        ],
        out_specs=pl.BlockSpec((None, NSA_GROUP * hd, NSA_TQ), lambda b, g, i: (b, g, i)),
        out_shape=jax.ShapeDtypeStruct((bsz, NSA_HEADS * hd, seq), F32),
        scratch_shapes=[pltpu.VMEM((n_blk, wq), F32)],
        compiler_params=_cparams("parallel", "parallel", "arbitrary"),
        name="nsa_prompt",
    )(slopes, yt, kcmp_t, vcmp_t, yt, yt, yt, yt, yt, gate_b_col, ov)


def _nsa_s_cmp_kernel(slopes_ref, q_ref, kc_ref, vc_ref, ov_ref, oc_ref, sel_ref, *, past, n_cmp, n_blk):
    g = pl.program_id(1)
    grp = NSA_GROUP
    nc = kc_ref.shape[-1]
    nb = ov_ref.shape[0]
    ci = lax.broadcasted_iota(jnp.int32, (1, nc), 1)
    c_end = ci * NSA_CMP_STRIDE + (NSA_CMP_LEN - 1)
    mask = (c_end <= past) & (ci < n_cmp)
    dist = (past - c_end).astype(F32)
    kc = _r16(kc_ref[...])
    vc = _r16(vc_ref[...])
    probs = []
    for r in range(grp):
        s = jnp.sum(kc * _r16(q_ref[r]), axis=0, keepdims=True)
        s = jnp.where(mask, s - slopes_ref[g * grp + r] * dist, NEG_INF)
        m = jnp.max(s, axis=-1, keepdims=True)
        p = jnp.where(mask, jnp.exp(s - m), 0.0)
        l = jnp.sum(p, axis=-1, keepdims=True)
        p = _r16(p / jnp.where(l > 0.0, l, 1.0))
        probs.append(p)
        oc_ref[r] = jnp.sum(vc * p, axis=-1, keepdims=True)
    p8 = jnp.concatenate(probs + [jnp.zeros((SUBLANES - grp, nc), F32)], axis=0).astype(BF16)
    imp = jnp.sum(lax.dot_general(p8, ov_ref[...].astype(BF16), _NT, preferred_element_type=F32),
                  axis=0, keepdims=True)

    bi = lax.broadcasted_iota(jnp.int32, (1, nb), 1)
    cur = past // NSA_SLC_BLOCK
    allowed = (bi <= cur) & (bi < n_blk)
    forced = (bi < NSA_INIT_BLOCKS) | (cur - bi < NSA_LOCAL_BLOCKS)
    val = jnp.where(allowed & forced, FORCE_SCORE, jnp.where(allowed, imp, NEG_INF))
    val = jnp.where(bi < n_blk, val, -jnp.inf)
    bj = lax.broadcasted_iota(jnp.int32, (nb, 1), 0)
    val_col = jnp.sum(jnp.where(bj == bi, val, 0.0), axis=-1, keepdims=True)
    beats = (val_col > val) | ((val_col == val) & (bj < bi))
    rank = jnp.sum(jnp.where(beats, 1.0, 0.0), axis=0, keepdims=True)
    lane = lax.broadcasted_iota(jnp.int32, (1, LANES), 1)
    out = jnp.full((1, LANES), -1, jnp.int32)
    for k in range(NSA_SLC_TOPN):
        hit = (rank == float(k)) & allowed
        idx = jnp.sum(jnp.where(hit, bi, 0), axis=-1, keepdims=True)
        cnt = jnp.sum(jnp.where(hit, 1, 0), axis=-1, keepdims=True)
        out = jnp.where(lane == k, jnp.where(cnt > 0, idx, -1), out)
    sel_ref[...] = out


def _nsa_sample_cmp(q, kcmp_t, vcmp_t, slopes, past, n_keys):
    bsz = q.shape[0]
    n_chunk = kcmp_t.shape[-1]
    n_cmp = max((n_keys - NSA_CMP_LEN) // NSA_CMP_STRIDE + 1, 1)
    n_blk = max(-(-n_keys // NSA_SLC_BLOCK), NSA_SLC_TOPN)
    nb_pad = -(-n_blk // LANES) * LANES
    ov = _nsa_overlap_t(n_chunk, nb_pad)
    qspec = pl.BlockSpec((None, None, NSA_GROUP, HEAD_DIM, 1), lambda b, g: (b, g, 0, 0, 0))
    cspec = pl.BlockSpec((None, None, HEAD_DIM, n_chunk), lambda b, g: (b, g, 0, 0))
    return pl.pallas_call(
        functools.partial(_nsa_s_cmp_kernel, past=past, n_cmp=n_cmp, n_blk=n_blk),
        grid=(bsz, NSA_KV_HEADS),
        in_specs=[pl.BlockSpec(memory_space=pltpu.SMEM), qspec, cspec, cspec,
                  pl.BlockSpec(ov.shape, lambda b, g: (0, 0))],
        out_specs=[qspec, pl.BlockSpec((None, None, 1, LANES), lambda b, g: (b, g, 0, 0))],
        out_shape=[jax.ShapeDtypeStruct(q.shape, F32),
                   jax.ShapeDtypeStruct((bsz, NSA_KV_HEADS, 1, LANES), jnp.int32)],
        compiler_params=_cparams("parallel", "parallel"),
        name="nsa_sample_cmp",
    )(slopes, q, kcmp_t, vcmp_t, ov)


def _nsa_s_attn_kernel(pt_ref, sel_ref, slopes_ref, q_ref, oc_ref, gt_ref, gb_ref, ksn_ref, vsn_ref, kwn_ref,
                       vwn_ref, kw_ref, vw_ref, *refs, past):
    n_sel = NSA_SLC_TOPN
    k_refs, v_refs, o_ref = refs[:n_sel], refs[n_sel:2 * n_sel], refs[2 * n_sel]
    b = pl.program_id(0)
    g = pl.program_id(1)
    grp = NSA_GROUP
    lane = lax.broadcasted_iota(jnp.int32, (1, LANES), 1)
    cur = past // NSA_SLC_BLOCK
    n_win = kw_ref.shape[-1]
    wpos = past - n_win + lax.broadcasted_iota(jnp.int32, (1, n_win), 1)
    wdist = past - wpos
    wmask = (wpos >= 0) & (wdist <= NSA_WINDOW)
    blks = [sel_ref[b, g * n_sel + k] for k in range(n_sel)]
    has_new = jnp.zeros((), jnp.int32)
    for blk in blks:
        has_new = jnp.maximum(has_new, (blk == cur).astype(jnp.int32))
    new_ok = jnp.full((1, 1), has_new, jnp.int32) > 0

    for r in range(grp):
        q = _r16(q_ref[r])
        slope = slopes_ref[g * grp + r]
        s_new = jnp.sum(_r16(ksn_ref[...]) * q, axis=0, keepdims=True)
        m = jnp.where(new_ok, s_new, NEG_INF)
        scores = []
        for k in range(n_sel):
            blk = blks[k]
            kpos = (blk // 2) * LANES + lane
            valid = (kpos // NSA_SLC_BLOCK == blk) & (blk >= 0) & (kpos < past)
            s = jnp.sum(_r16(k_refs[k][...]) * q, axis=0, keepdims=True)
            s = jnp.where(valid, s - slope * (past - kpos).astype(F32), NEG_INF)
            scores.append((s, valid))
            m = jnp.maximum(m, jnp.max(s, axis=-1, keepdims=True))
        p_new = jnp.where(new_ok, jnp.exp(s_new - m), 0.0)
        probs = [jnp.where(valid, jnp.exp(s - m), 0.0) for s, valid in scores]
        l = p_new
        for p in probs:
            l = l + jnp.sum(p, axis=-1, keepdims=True)
        inv_l = 1.0 / jnp.where(l > 0.0, l, 1.0)
        o_s = _r16(p_new * inv_l) * _r16(vsn_ref[...])
        for p, v_ref in zip(probs, v_refs):
            o_s = o_s + jnp.sum(_r16(v_ref[...]) * _r16(p * inv_l), axis=-1, keepdims=True)
        sw = jnp.sum(_r16(kw_ref[...]) * q, axis=0, keepdims=True)
        sw = jnp.where(wmask, sw - slope * wdist.astype(F32), NEG_INF)
        sw_new = jnp.sum(_r16(kwn_ref[...]) * q, axis=0, keepdims=True)
        mw = jnp.maximum(jnp.max(sw, axis=-1, keepdims=True), sw_new)
        pw = jnp.where(wmask, jnp.exp(sw - mw), 0.0)
        pw_new = jnp.exp(sw_new - mw)
        inv_lw = 1.0 / (jnp.sum(pw, axis=-1, keepdims=True) + pw_new)
        o_w = (jnp.sum(_r16(vw_ref[...]) * _r16(pw * inv_lw), axis=-1, keepdims=True)
               + _r16(pw_new * inv_lw) * _r16(vwn_ref[...]))
        gates = [1.0 / (1.0 + jnp.exp(-(gt_ref[r * 3 + br:r * 3 + br + 1, :] + gb_ref[r * 3 + br:r * 3 + br + 1, :])))
                 for br in range(3)]
        o_ref[r] = gates[0] * oc_ref[r] + gates[1] * o_s + gates[2] * o_w


def _nsa_sample_attn(page_table, sel, slopes, q, o_c, gt, gb, new_rows, kwin_t, vwin_t, kpool_t, vpool_t):
    bsz, n_pages = page_table.shape
    past = n_pages * LANES
    assert past % LANES == 0
    n_sel = NSA_SLC_TOPN
    n_win = kwin_t.shape[-1]

    def im(f):
        return lambda b, g, pt, sl: f(b, g, pt, sl)

    qspec = pl.BlockSpec((None, None, NSA_GROUP, HEAD_DIM, 1), lambda b, g, pt, sl: (b, g, 0, 0, 0))
    col = pl.BlockSpec((None, None, HEAD_DIM, 1), lambda b, g, pt, sl: (b, g, 0, 0))
    win = pl.BlockSpec((None, None, HEAD_DIM, n_win), lambda b, g, pt, sl: (b, g, 0, 0))

    def page_spec(k):
        def index(b, g, pt, sl):
            page = jnp.clip(sl[b, g * n_sel + k] // 2, 0, n_pages - 1)
            return (pt[b, page], g, 0, 0)
        return pl.BlockSpec((None, None, HEAD_DIM, LANES), index)

    pages = [page_spec(k) for k in range(n_sel)]
    return pl.pallas_call(
        functools.partial(_nsa_s_attn_kernel, past=past),
        grid_spec=pltpu.PrefetchScalarGridSpec(
            num_scalar_prefetch=2,
            grid=(bsz, NSA_KV_HEADS),
            in_specs=[pl.BlockSpec(memory_space=pltpu.SMEM), qspec, qspec,
                      pl.BlockSpec((None, None, 3 * NSA_GROUP, 1), lambda b, g, pt, sl: (b, g, 0, 0)),
                      pl.BlockSpec((None, 3 * NSA_GROUP, 1), lambda b, g, pt, sl: (g, 0, 0)),
                      col, col, col, col, win, win] + pages + pages,
            out_specs=qspec,
        ),
        out_shape=jax.ShapeDtypeStruct(q.shape, F32),
        compiler_params=_cparams("parallel", "parallel"),
        name="nsa_sample_attn",
    )(page_table, sel, slopes, q, o_c, gt, gb, *new_rows, kwin_t, vwin_t,
      *([kpool_t] * n_sel), *([vpool_t] * n_sel))


MOE_TOKEN_TILE = 256
MOE_ROW_BLOCK = 512


def _router_kernel(x_ref, g_ref, rt_ref, h_ref, idx_ref, gate_ref):
    x = x_ref[...]
    ms = jnp.mean(x * x, axis=-1, keepdims=True)
    h = x * lax.rsqrt(ms + RMS_EPS) * g_ref[...]
    h_ref[...] = h
    logits = lax.dot_general(rt_ref[...].astype(BF16), h.astype(BF16), _NT, preferred_element_type=F32)
    e_id = lax.broadcasted_iota(jnp.int32, logits.shape, 0)
    v1 = jnp.max(logits, axis=0, keepdims=True)
    i1 = jnp.min(jnp.where(logits == v1, e_id, N_EXPERTS), axis=0, keepdims=True)
    rest = jnp.where(e_id == i1, -jnp.inf, logits)
    v2 = jnp.max(rest, axis=0, keepdims=True)
    i2 = jnp.min(jnp.where(rest == v2, e_id, N_EXPERTS), axis=0, keepdims=True)
    e2 = jnp.exp(v2 - v1)
    idx_ref[...] = jnp.concatenate([i1, i2], axis=0)
    gate_ref[...] = jnp.concatenate([1.0 / (1.0 + e2), e2 / (1.0 + e2)], axis=0)


def _router(x, g, router_t):
    n, d = x.shape
    tm = MOE_TOKEN_TILE
    return pl.pallas_call(
        _router_kernel,
        grid=(n // tm,),
        in_specs=[pl.BlockSpec((tm, d), lambda i: (i, 0)), pl.BlockSpec((1, d), lambda i: (0, 0)),
                  pl.BlockSpec(router_t.shape, lambda i: (0, 0))],
        out_specs=[pl.BlockSpec((tm, d), lambda i: (i, 0)), pl.BlockSpec((TOP_K, tm), lambda i: (0, i)),
                   pl.BlockSpec((TOP_K, tm), lambda i: (0, i))],
        out_shape=[jax.ShapeDtypeStruct((n, d), F32), jax.ShapeDtypeStruct((TOP_K, n), jnp.int32),
                   jax.ShapeDtypeStruct((TOP_K, n), F32)],
        compiler_params=_cparams("parallel"),
        name="moe_router",
    )(x, g, router_t)


def _gather_rows_kernel(idx_ref, src_ref, o_ref, sem):
    rows = o_ref.shape[0]

    def row_copy(r):
        return pltpu.make_async_copy(src_ref.at[pl.ds(idx_ref[0, r], 1)], o_ref.at[pl.ds(r, 1)], sem)

    def start(r, c):
        row_copy(r).start()
        return c

    def wait(r, c):
        row_copy(r).wait()
        return c

    lax.fori_loop(0, rows, start, 0, unroll=8)
    lax.fori_loop(0, rows, wait, 0, unroll=8)


def _gather_rows(src, idx, block):
    m = idx.shape[0]
    d = src.shape[1]
    return pl.pallas_call(
        _gather_rows_kernel,
        grid=(m // block,),
        in_specs=[pl.BlockSpec((None, 1, block), lambda i: (i, 0, 0), memory_space=pltpu.SMEM),
                  pl.BlockSpec(memory_space=pl.ANY)],
        out_specs=pl.BlockSpec((block, d), lambda i: (i, 0)),
        out_shape=jax.ShapeDtypeStruct((m, d), src.dtype),
        scratch_shapes=[pltpu.SemaphoreType.DMA(())],
        compiler_params=_cparams("arbitrary"),
        name="gather_rows",
    )(idx.reshape(m // block, 1, block), src)


def _experts_kernel(be_ref, act_ref, x_ref, wg_ref, wu_ref, wd_ref, o_ref, acc_scr):
    i = pl.program_id(0)
    j = pl.program_id(1)

    @pl.when(act_ref[i] > 0)
    def _():
        @pl.when(j == 0)
        def _():
            acc_scr[...] = jnp.zeros_like(acc_scr)

        xb = x_ref[...].astype(BF16)
        gate = jnp.dot(xb, wg_ref[...], preferred_element_type=F32)
        up = jnp.dot(xb, wu_ref[...], preferred_element_type=F32)
        acc_scr[...] += jnp.dot((_silu(gate) * up).astype(BF16), wd_ref[...], preferred_element_type=F32)

    @pl.when(j == pl.num_programs(1) - 1)
    def _():
        o_ref[...] = jnp.where(act_ref[i] > 0, acc_scr[...], 0.0)


def _experts(block_expert, block_active, xg, wg, wu, wd):
    m, d = xg.shape
    rb = MOE_ROW_BLOCK
    f = wg.shape[2]
    tf = _largest_tile(f, 1792, LANES)
    return pl.pallas_call(
        _experts_kernel,
        grid_spec=pltpu.PrefetchScalarGridSpec(
            num_scalar_prefetch=2,
            grid=(m // rb, f // tf),
            in_specs=[
                pl.BlockSpec((rb, d), lambda i, j, be, act: (i, 0)),
                pl.BlockSpec((None, d, tf), lambda i, j, be, act: (be[i], 0, j)),
                pl.BlockSpec((None, d, tf), lambda i, j, be, act: (be[i], 0, j)),
                pl.BlockSpec((None, tf, d), lambda i, j, be, act: (be[i], j, 0)),
            ],
            out_specs=pl.BlockSpec((rb, d), lambda i, j, be, act: (i, 0)),
            scratch_shapes=[pltpu.VMEM((rb, d), F32)],
        ),
        out_shape=jax.ShapeDtypeStruct((m, d), F32),
        compiler_params=_cparams("arbitrary", "arbitrary"),
        name="moe_experts",
    )(block_expert, block_active, xg, wg, wu, wd)


def _combine_kernel(x_ref, y0_ref, y1_ref, g0_ref, g1_ref, o_ref):
    o_ref[...] = x_ref[...] + (g0_ref[...] * y0_ref[...] + g1_ref[...] * y1_ref[...])


def _combine(x, y2, gates):
    n, d = x.shape
    tm = MOE_TOKEN_TILE
    nt = n // tm
    g0 = gates[0].reshape(n, 1)
    g1 = gates[1].reshape(n, 1)
    return pl.pallas_call(
        _combine_kernel,
        grid=(nt,),
        in_specs=[pl.BlockSpec((tm, d), lambda i: (i, 0)), pl.BlockSpec((tm, d), lambda i: (i, 0)),
                  pl.BlockSpec((tm, d), lambda i: (i + nt, 0)), pl.BlockSpec((tm, 1), lambda i: (i, 0)),
                  pl.BlockSpec((tm, 1), lambda i: (i, 0))],
        out_specs=pl.BlockSpec((tm, d), lambda i: (i, 0)),
        out_shape=jax.ShapeDtypeStruct((n, d), F32),
        compiler_params=_cparams("parallel"),
        name="moe_combine",
    )(x, y2, y2, g0, g1)


def _moe(x, norm, router_t, wg, wu, wd):
    n = x.shape[0]
    rb = MOE_ROW_BLOCK
    h, idx, gates = _router(x, norm.reshape(1, -1), router_t)
    expert = idx.reshape(-1)
    n_asg = expert.shape[0]
    onehot = (expert[:, None] == jnp.arange(N_EXPERTS, dtype=jnp.int32)[None, :]).astype(jnp.int32)
    pos = jnp.take_along_axis(jnp.cumsum(onehot, axis=0), expert[:, None], axis=1)[:, 0] - 1
    counts = jnp.sum(onehot, axis=0)
    padded = (counts + rb - 1) // rb * rb
    p_end = jnp.cumsum(padded)
    slot = (p_end - padded)[expert] + pos
    n_blocks = -(-n_asg // rb) + N_EXPERTS
    token = jnp.tile(jnp.arange(n, dtype=jnp.int32), TOP_K)
    tok_of_slot = jnp.zeros((n_blocks * rb,), jnp.int32).at[slot].set(token)
    starts = jnp.arange(n_blocks, dtype=jnp.int32) * rb
    block_expert = jnp.minimum(jnp.searchsorted(p_end, starts, side="right"), N_EXPERTS - 1).astype(jnp.int32)
    block_active = (starts < p_end[-1]).astype(jnp.int32)
    xg = _gather_rows(h, tok_of_slot, rb)
    yg = _experts(block_expert, block_active, xg, wg, wu, wd)
    y2 = _gather_rows(yg, slot.astype(jnp.int32), MOE_TOKEN_TILE)
    return _combine(x, y2, gates)


def _col(v):
    return v.reshape(-1, 1).astype(F32)


def _gm_weights(w_in, q_norm, k_norm):
    wt = w_in.T
    n_ga0 = 2 * GLA_HEADS * GLA_DK + GLA_HEADS * GLA_DV
    wt = jnp.concatenate([wt[:n_ga0], wt[n_ga0 + GLA_GATE_RANK:], wt[n_ga0:n_ga0 + GLA_GATE_RANK],
                          jnp.zeros((LANES - GLA_GATE_RANK, wt.shape[1]), wt.dtype)], axis=0).astype(BF16)
    ones = lambda n: jnp.ones((n,), F32)
    nm = MOBA_HEADS * HEAD_DIM
    gain = jnp.concatenate([ones(256) * GLA_DK ** -0.5, ones(256), ones(512), ones(512),
                            jnp.tile(q_norm, MOBA_HEADS) * HEAD_DIM ** -0.5, jnp.tile(k_norm, MOBA_HEADS),
                            ones(nm), ones(LANES)])
    nflag = jnp.concatenate([jnp.zeros((1536,), F32), ones(2 * nm), jnp.zeros((nm + LANES,), F32)])
    return wt, _col(nflag), _col(gain)


GM_ROWS = dict(qg=0, kg=256, vg=512, rg=1024, qm=1536, km=2048, vm=2560, ga=3072)


def _gm_prompt(xp, bsz, seq, norm_mix, wts, gate_w, gate_b, out_norm, w_out, tm):
    wt, nflag, gain = wts
    r = GM_ROWS
    yt = _proj_t(xp, 0, bsz, seq, norm_mix.reshape(1, -1), wt, nflag, gain, tm)
    gw_t = gate_w.T.reshape(GLA_HEADS, GLA_DK, GLA_GATE_RANK)
    gb = gate_b.reshape(GLA_HEADS, GLA_DK, 1)
    on = out_norm.reshape(GLA_HEADS, GLA_DV, 1)
    og, s_t = _gla_prompt(yt, (r["qg"], r["kg"], r["vg"], r["rg"], r["ga"]), gw_t, gb, on)
    om = _moba_prompt(yt, _alibi_slopes(MOBA_HEADS), r["qm"], r["km"], r["vm"])
    xp = _outproj_t(xp, 0, [og, om], w_out.astype(BF16), tm)
    return xp, s_t, yt


NS_ROWS = dict(q=0, kc=1024, vc=1280, ks=1536, vs=1792, kw=2048, vw=2304, g=2560)


def _ns_weights(w_in, q_norm, ks_norm, kw_norm):
    wt = w_in.T
    n_g = NSA_HEADS * 3
    wt = jnp.concatenate([wt, jnp.zeros((LANES - n_g, wt.shape[1]), wt.dtype)], axis=0).astype(BF16)
    ones = lambda n: jnp.ones((n,), F32)
    zeros = lambda n: jnp.zeros((n,), F32)
    kvw = NSA_KV_HEADS * HEAD_DIM
    gain = jnp.concatenate([jnp.tile(q_norm, NSA_HEADS) * HEAD_DIM ** -0.5, ones(2 * kvw),
                            jnp.tile(ks_norm, NSA_KV_HEADS), ones(kvw),
                            jnp.tile(kw_norm, NSA_KV_HEADS), ones(kvw), ones(LANES)])
    nflag = jnp.concatenate([ones(NSA_HEADS * HEAD_DIM), zeros(2 * kvw), ones(kvw), zeros(kvw), ones(kvw),
                             zeros(kvw), zeros(LANES)])
    return wt, _col(nflag), _col(gain)


def _ns_prompt(xp, bsz, seq, norm_mix, wts, gate_b, kc_norm, cmpk, cmpv, w_out, tm):
    wt, nflag, gain = wts
    r = NS_ROWS
    yt = _proj_t(xp, 0, bsz, seq, norm_mix.reshape(1, -1), wt, nflag, gain, tm)
    kcmp_t = _cmp_stage2(_cmp_stage1_dense(yt, r["kc"], _cmp_w1cat(cmpk[1])), *cmpk, kc_norm)
    vcmp_t = _cmp_stage2(_cmp_stage1_dense(yt, r["vc"], _cmp_w1cat(cmpv[1])), *cmpv, None)
    gb = jnp.pad(gate_b, (0, LANES - gate_b.shape[0])).reshape(LANES, 1)
    o_t = _nsa_prompt(yt, kcmp_t, vcmp_t, _alibi_slopes(NSA_HEADS),
                      (r["q"], r["ks"], r["vs"], r["kw"], r["vw"], r["g"]), gb)
    xp = _outproj_t(xp, 0, [o_t], w_out.astype(BF16), tm)
    return xp, yt


def _sample_rows(xs, norm, wts):
    n = xs.shape[0]
    wt, nflag, gain = wts
    xs_pad = jnp.pad(xs, ((0, LANES - n), (0, 0)))
    yt = _proj_t(xs_pad, 0, 1, LANES, norm.reshape(1, -1), wt, nflag, gain, LANES)
    return yt[0, :, :n].T


def _ns_sample(xs, norm_mix, wts, gate_b, kc_norm, cmpk, cmpv, w_out, pools, win_k, win_v, page_table):
    n = xs.shape[0]
    r = NS_ROWS
    past = page_table.shape[1] * LANES
    ys = _sample_rows(xs, norm_mix, wts)
    kvw = NSA_KV_HEADS * HEAD_DIM
    new = {nm: ys[:, r[nm]:r[nm] + kvw].reshape(n, NSA_KV_HEADS, HEAD_DIM) for nm in ("kc", "vc", "ks", "vs", "kw", "vw")}
    colv = lambda a: a.reshape(n, NSA_KV_HEADS, HEAD_DIM, 1)
    q = ys[:, :NSA_HEADS * HEAD_DIM].reshape(n, NSA_KV_HEADS, NSA_GROUP, HEAD_DIM, 1)
    gt = ys[:, r["g"]:r["g"] + 3 * NSA_HEADS].reshape(n, NSA_KV_HEADS, 3 * NSA_GROUP, 1)
    gb = gate_b.reshape(NSA_KV_HEADS, 3 * NSA_GROUP, 1)
    view = lambda a: jnp.transpose(a, (0, 2, 3, 1))
    ck_t, cv_t, sk_t, sv_t = (view(p) for p in pools)
    kcmp_t = _cmp_stage2(_cmp_stage1_paged(page_table, ck_t, _cmp_w1cat(cmpk[1])), *cmpk, kc_norm)
    vcmp_t = _cmp_stage2(_cmp_stage1_paged(page_table, cv_t, _cmp_w1cat(cmpv[1])), *cmpv, None)
    slopes = _alibi_slopes(NSA_HEADS)
    o_c, sel = _nsa_sample_cmp(q, kcmp_t, vcmp_t, slopes, past, past + 1)
    sel = sel[:, :, 0, :NSA_SLC_TOPN].reshape(n, NSA_KV_HEADS * NSA_SLC_TOPN)
    o = _nsa_sample_attn(page_table, sel, slopes, q, o_c, gt, gb,
                         [colv(new[nm]) for nm in ("ks", "vs", "kw", "vw")], view(win_k), view(win_v), sk_t, sv_t)
    xs = _outproj_r(xs, o.reshape(n, NSA_HEADS * HEAD_DIM), w_out.astype(BF16))
    return xs, new


def _gm_sample(xs, norm_mix, wts, gate_w, gate_b, out_norm, w_out, state, pool_k, pool_v, page_table):
    n = xs.shape[0]
    r = GM_ROWS
    ys = _sample_rows(xs, norm_mix, wts)
    hk, hv, nm = GLA_HEADS * GLA_DK, GLA_HEADS * GLA_DV, MOBA_HEADS * HEAD_DIM
    col = lambda r0, heads, dim: ys[:, r0:r0 + heads * dim].reshape(n, heads, dim, 1)
    row = lambda r0, heads, dim: ys[:, r0:r0 + heads * dim].reshape(n, heads, 1, dim)
    gw_t = gate_w.T.reshape(GLA_HEADS, GLA_DK, GLA_GATE_RANK)
    og, s_new = _gla_sample(
        col(r["qg"], GLA_HEADS, GLA_DK), col(r["kg"], GLA_HEADS, GLA_DK), row(r["vg"], GLA_HEADS, GLA_DV),
        row(r["rg"], GLA_HEADS, GLA_DV), ys[:, r["ga"]:r["ga"] + GLA_GATE_RANK].reshape(n, 1, 1, GLA_GATE_RANK),
        gw_t, gate_b.reshape(GLA_HEADS, GLA_DK, 1), out_norm.reshape(GLA_HEADS, 1, GLA_DV), state)
    qm, km, vm = (col(r[nme], MOBA_HEADS, HEAD_DIM) for nme in ("qm", "km", "vm"))
    kpool_t = jnp.transpose(pool_k, (0, 2, 3, 1))
    vpool_t = jnp.transpose(pool_v, (0, 2, 3, 1))
    _, top = _moba_gates(page_table, qm, kpool_t)
    top = top[:, :, :MOBA_TOPK].reshape(n, MOBA_HEADS * MOBA_TOPK)
    om = _moba_sample(page_table, top, _alibi_slopes(MOBA_HEADS), qm, km, vm, kpool_t, vpool_t)
    a = jnp.concatenate([og.reshape(n, hv), om.reshape(n, nm)], axis=1)
    xs = _outproj_r(xs, a, w_out.astype(BF16))
    return xs, s_new, km.reshape(n, MOBA_HEADS, HEAD_DIM), vm.reshape(n, MOBA_HEADS, HEAD_DIM)


def _rows_to_cache(yt, row0, heads):
    bsz, _, seq = yt.shape
    a = yt[:, row0:row0 + heads * HEAD_DIM].reshape(bsz, heads, HEAD_DIM, seq)
    return jnp.transpose(a, (0, 3, 1, 2))


def kernel(x_prompt, x_sample, state_gla, cache_moba_k, cache_moba_v, cache_nsa_cmp_k, cache_nsa_cmp_v, cache_nsa_slc_k, cache_nsa_slc_v, state_nsa_win_k, state_nsa_win_v, page_table, gm_norm_mix, gm_w_in, gm_gla_gate_w, gm_gla_gate_b, gm_gla_out_norm, gm_moba_q_norm, gm_moba_k_norm, gm_w_out, gm_norm_ffn, gm_ffn_gate, gm_ffn_up, gm_ffn_down, ns_norm_mix, ns_w_in, ns_gate_b, ns_q_norm, ns_kcmp_norm, ns_kslc_norm, ns_kwin_norm, ns_cmpk_pe, ns_cmpk_w1, ns_cmpk_b1, ns_cmpk_w2, ns_cmpk_b2, ns_cmpv_pe, ns_cmpv_w1, ns_cmpv_b1, ns_cmpv_w2, ns_cmpv_b2, ns_w_out, ns_norm_ffn, ns_router, ns_exp_gate, ns_exp_up, ns_exp_down):
    bsz, seq, d = x_prompt.shape
    n_s = x_sample.shape[0]
    assert x_sample.shape[1] == 1 and gm_w_in.shape[0] == 1 and ns_w_in.shape[0] == 1
    tm = _largest_tile(seq, 512, LANES)
    xp = x_prompt.reshape(bsz * seq, d)
    xs = x_sample.reshape(n_s, d)

    i = 0
    wts = _gm_weights(gm_w_in[i], gm_moba_q_norm[i], gm_moba_k_norm[i])
    xp, gla_p, yt0 = _gm_prompt(xp, bsz, seq, gm_norm_mix[i], wts, gm_gla_gate_w[i], gm_gla_gate_b[i],
                                gm_gla_out_norm[i], gm_w_out[i], tm)
    xs, gla_s, mk_s, mv_s = _gm_sample(xs, gm_norm_mix[i], wts, gm_gla_gate_w[i], gm_gla_gate_b[i],
                                       gm_gla_out_norm[i], gm_w_out[i], state_gla[i], cache_moba_k[i],
                                       cache_moba_v[i], page_table)
    ffn = (gm_norm_ffn[i].reshape(1, -1), gm_ffn_gate[i].astype(BF16), gm_ffn_up[i].astype(BF16),
           gm_ffn_down[i].astype(BF16))
    xp = _ffn(xp, *ffn, tm)
    xs = _ffn(xs, *ffn, n_s)

    wts = _ns_weights(ns_w_in[i], ns_q_norm[i], ns_kslc_norm[i], ns_kwin_norm[i])
    cmpk = (ns_cmpk_pe[i], ns_cmpk_w1[i], ns_cmpk_b1[i], ns_cmpk_w2[i], ns_cmpk_b2[i])
    cmpv = (ns_cmpv_pe[i], ns_cmpv_w1[i], ns_cmpv_b1[i], ns_cmpv_w2[i], ns_cmpv_b2[i])
    xp, yt1 = _ns_prompt(xp, bsz, seq, ns_norm_mix[i], wts, ns_gate_b[i], ns_kcmp_norm[i], cmpk, cmpv,
                         ns_w_out[i], tm)
    pools = (cache_nsa_cmp_k[i], cache_nsa_cmp_v[i], cache_nsa_slc_k[i], cache_nsa_slc_v[i])
    xs, new = _ns_sample(xs, ns_norm_mix[i], wts, ns_gate_b[i], ns_kcmp_norm[i], cmpk, cmpv, ns_w_out[i], pools,
                         state_nsa_win_k[i], state_nsa_win_v[i], page_table)
    n_tok = bsz * seq + n_s
    n_pad = -(-n_tok // MOE_TOKEN_TILE) * MOE_TOKEN_TILE
    x_all = jnp.concatenate([xp, xs, jnp.zeros((n_pad - n_tok, d), F32)], axis=0)
    x_all = _moe(x_all, ns_norm_ffn[i], ns_router[i].T, ns_exp_gate[i].astype(BF16), ns_exp_up[i].astype(BF16),
                 ns_exp_down[i].astype(BF16))
    y_prompt = x_all[:bsz * seq].reshape(bsz, seq, d)
    y_sample = x_all[bsz * seq:n_tok].reshape(n_s, 1, d)

    r0, r1 = GM_ROWS, NS_ROWS
    n_win = min(NSA_WINDOW, seq)
    ns_p = {nm: _rows_to_cache(yt1, r1[nm], NSA_KV_HEADS) for nm in ("kc", "vc", "ks", "vs", "kw", "vw")}
    win_s = lambda state, row: jnp.concatenate([state, row[:, None]], axis=1)[:, -min(NSA_WINDOW, state.shape[1] + 1):]
    lead = lambda a: a[None]
    tok = lambda a: a[None, :, None]
    return (y_prompt, y_sample, lead(gla_p), lead(gla_s),
            lead(_rows_to_cache(yt0, r0["km"], MOBA_HEADS)), lead(_rows_to_cache(yt0, r0["vm"], MOBA_HEADS)),
            tok(mk_s), tok(mv_s),
            lead(ns_p["kc"]), lead(ns_p["vc"]), lead(ns_p["ks"]), lead(ns_p["vs"]),
            lead(ns_p["kw"][:, -n_win:]), lead(ns_p["vw"][:, -n_win:]),
            tok(new["kc"]), tok(new["vc"]), tok(new["ks"]), tok(new["vs"]),
            lead(win_s(state_nsa_win_k[i], new["kw"])), lead(win_s(state_nsa_win_v[i], new["vw"])))
```

```python
import functools

import numpy as np
import jax
import jax.numpy as jnp
from jax import lax
from jax.experimental import pallas as pl
from jax.experimental.pallas import tpu as pltpu

F32 = jnp.float32
BF16 = jnp.bfloat16

HEAD_DIM = 64
GLA_HEADS, GLA_DK, GLA_DV, GLA_GATE_RANK, GLA_TAU, GLA_CHUNK = 4, 64, 128, 16, 16.0, 16
MOBA_HEADS, MOBA_BLOCK, MOBA_TOPK = 8, 256, 3
NSA_HEADS, NSA_KV_HEADS = 16, 4
NSA_GROUP = NSA_HEADS // NSA_KV_HEADS
NSA_CMP_LEN, NSA_CMP_STRIDE, NSA_CMP_HIDDEN = 32, 16, 64
NSA_SLC_BLOCK, NSA_SLC_TOPN, NSA_INIT_BLOCKS, NSA_LOCAL_BLOCKS, NSA_WINDOW = 64, 16, 1, 2, 512
N_EXPERTS, TOP_K = 8, 2
QUERY_BLOCK = 16
RMS_EPS = 1e-6
NEG_INF = -1e30
FORCE_SCORE = 1e9

LANES = 128
SUBLANES = 8
VMEM_LIMIT_BYTES = 56 * 1024 * 1024

_NT = (((1,), (1,)), ((), ()))
_TN = (((0,), (0,)), ((), ()))
_HI = lax.Precision.HIGHEST


def _cparams(*sem):
    return pltpu.CompilerParams(dimension_semantics=sem, vmem_limit_bytes=VMEM_LIMIT_BYTES)


def _largest_tile(n, cap, mult):
    t = (cap // mult) * mult
    while t > mult and n % t:
        t -= mult
    assert n % t == 0, (n, cap, mult)
    return t


def _alibi_slopes(n):
    return 2.0 ** (-8.0 * jnp.arange(1, n + 1, dtype=F32) / n)


def _silu(x):
    return x * (1.0 / (1.0 + jnp.exp(-x)))


def _loop_tiles(lo, hi, fn, carry, groups=(4, 2, 1)):
    for group in groups:
        n = (hi - lo) // group

        def trip(i, c, lo=lo, group=group):
            for u in range(group):
                c = fn(lo + group * i + u, c)
            return c

        carry = lax.fori_loop(0, n, trip, carry)
        lo = lo + n * group
    return carry


def _r16(x):
    return x.astype(BF16).astype(F32)


def _proj_t_kernel(x_ref, g_ref, wt_ref, nflag_ref, gain_ref, o_ref, h_scr):
    @pl.when(pl.program_id(1) == 0)
    def _():
        x = x_ref[...]
        ms = jnp.mean(x * x, axis=-1, keepdims=True)
        h_scr[...] = (x * lax.rsqrt(ms + RMS_EPS) * g_ref[...]).astype(BF16)

    y = lax.dot_general(wt_ref[...], h_scr[...], _NT, preferred_element_type=F32)
    tc, tm = y.shape
    y3 = y.reshape(tc // HEAD_DIM, HEAD_DIM, tm)
    ms = jnp.mean(y3 * y3, axis=1, keepdims=True)
    nf = nflag_ref[...].reshape(tc // HEAD_DIM, HEAD_DIM, 1)
    gn = gain_ref[...].reshape(tc // HEAD_DIM, HEAD_DIM, 1)
    scale = jnp.where(nf > 0.0, lax.rsqrt(ms + RMS_EPS), 1.0) * gn
    o_ref[...] = (y3 * scale).reshape(tc, tm)


def _proj_t(x, row0, bsz, seq, g, wt, nflag, gain, tm):
    d = x.shape[1]
    c = wt.shape[0]
    tc = _largest_tile(c, 640, LANES)
    nt = seq // tm
    return pl.pallas_call(
        _proj_t_kernel,
        grid=(bsz * nt, c // tc),
        in_specs=[
            pl.BlockSpec((tm, d), lambda i, j: (row0 // tm + i, 0)),
            pl.BlockSpec((1, d), lambda i, j: (0, 0)),
            pl.BlockSpec((tc, d), lambda i, j: (j, 0)),
            pl.BlockSpec((tc, 1), lambda i, j: (j, 0)),
            pl.BlockSpec((tc, 1), lambda i, j: (j, 0)),
        ],
        out_specs=pl.BlockSpec((None, tc, tm), lambda i, j: (i // nt, j, i % nt)),
        out_shape=jax.ShapeDtypeStruct((bsz, c, seq), F32),
        scratch_shapes=[pltpu.VMEM((tm, d), BF16)],
        compiler_params=_cparams("parallel", "arbitrary"),
        name="proj_t",
    )(x, g, wt, nflag, gain)


def _outproj_t_kernel(*refs, n_in):
    x_ref, a_refs, w_ref, o_ref = refs[0], refs[1:1 + n_in], refs[1 + n_in], refs[2 + n_in]
    a = jnp.concatenate([r[...].astype(BF16) for r in a_refs], axis=0)
    o_ref[...] = x_ref[...] + lax.dot_general(a, w_ref[...], _TN, preferred_element_type=F32)


def _outproj_t(x, row0, a_list, w, tm):
    d = x.shape[1]
    bsz, _, seq = a_list[0].shape
    nt = seq // tm
    a_specs = [pl.BlockSpec((None, a.shape[1], tm), lambda i: (i // nt, 0, i % nt)) for a in a_list]
    return pl.pallas_call(
        functools.partial(_outproj_t_kernel, n_in=len(a_list)),
        grid=(bsz * nt,),
        in_specs=[pl.BlockSpec((tm, d), lambda i: (row0 // tm + i, 0))] + a_specs
        + [pl.BlockSpec(w.shape, lambda i: (0, 0))],
        out_specs=pl.BlockSpec((tm, d), lambda i: (i, 0)),
        out_shape=jax.ShapeDtypeStruct((bsz * seq, d), F32),
        compiler_params=_cparams("parallel"),
        name="outproj_t",
    )(x, *a_list, w)


def _outproj_r_kernel(x_ref, a_ref, w_ref, o_ref):
    o_ref[...] = x_ref[...] + jnp.dot(a_ref[...].astype(BF16), w_ref[...], preferred_element_type=F32)


def _outproj_r(x, a, w):
    return pl.pallas_call(
        _outproj_r_kernel,
        out_shape=jax.ShapeDtypeStruct(x.shape, F32),
        compiler_params=pltpu.CompilerParams(vmem_limit_bytes=VMEM_LIMIT_BYTES),
        name="outproj_r",
    )(x, a, w)


def _ffn_kernel(x_ref, g_ref, wg_ref, wu_ref, wd_ref, o_ref, h_scr, acc_scr):
    j = pl.program_id(1)

    @pl.when(j == 0)
    def _():
        x = x_ref[...]
        ms = jnp.mean(x * x, axis=-1, keepdims=True)
        h_scr[...] = (x * lax.rsqrt(ms + RMS_EPS) * g_ref[...]).astype(BF16)
        acc_scr[...] = jnp.zeros_like(acc_scr)

    h = h_scr[...]
    gate = jnp.dot(h, wg_ref[...], preferred_element_type=F32)
    up = jnp.dot(h, wu_ref[...], preferred_element_type=F32)
    act = (_silu(gate) * up).astype(BF16)
    acc_scr[...] += jnp.dot(act, wd_ref[...], preferred_element_type=F32)

    @pl.when(j == pl.num_programs(1) - 1)
    def _():
        o_ref[...] = x_ref[...] + acc_scr[...]


def _ffn(x, g, wg, wu, wd, tm):
    n, d = x.shape
    f = wg.shape[1]
    tf = _largest_tile(f, 1536, LANES)
    return pl.pallas_call(
        _ffn_kernel,
        grid=(n // tm, f // tf),
        in_specs=[
            pl.BlockSpec((tm, d), lambda i, j: (i, 0)),
            pl.BlockSpec((1, d), lambda i, j: (0, 0)),
            pl.BlockSpec((d, tf), lambda i, j: (0, j)),
            pl.BlockSpec((d, tf), lambda i, j: (0, j)),
            pl.BlockSpec((tf, d), lambda i, j: (j, 0)),
        ],
        out_specs=pl.BlockSpec((tm, d), lambda i, j: (i, 0)),
        out_shape=jax.ShapeDtypeStruct((n, d), F32),
        scratch_shapes=[pltpu.VMEM((tm, d), BF16), pltpu.VMEM((tm, d), F32)],
        compiler_params=_cparams("parallel", "arbitrary"),
        name="ffn",
    )(x, g, wg, wu, wd)


def _moba_p_kernel(slopes_ref, q_ref, k_ref, v_ref, o_ref, sel_scr, s_scr, *, n_blk):
    h = pl.program_id(1)
    qi = pl.program_id(2)
    blk = MOBA_BLOCK
    slope = slopes_ref[h]
    q = q_ref[...]
    qb = q.astype(BF16)

    gates = []
    for j in range(n_blk):
        kmean = jnp.mean(k_ref[:, j * blk:(j + 1) * blk], axis=1, keepdims=True)
        gates.append(jnp.sum(qb.astype(F32) * _r16(kmean), axis=0, keepdims=True))
    for j in range(n_blk):
        cnt = jnp.zeros_like(gates[0])
        for j2 in range(n_blk):
            if j2 == j:
                continue
            beats = (gates[j2] >= gates[j]) if j2 < j else (gates[j2] > gates[j])
            cnt = cnt + jnp.where(beats, 1.0, 0.0) * (j2 < qi).astype(F32)
        sel_scr[j:j + 1, :] = jnp.where(cnt < float(MOBA_TOPK), 0.0, -NEG_INF)

    sub = lax.broadcasted_iota(jnp.int32, (blk, blk), 0)
    lane = lax.broadcasted_iota(jnp.int32, (blk, blk), 1)
    rel = (lane - sub).astype(F32)
    alibi = slope * rel

    def scores(j, penalty, row_const, m):
        start = pl.multiple_of(j * blk, blk)
        kj = k_ref[:, pl.ds(start, blk)].astype(BF16)
        s = lax.dot_general(kj, qb, _TN, preferred_element_type=F32) - penalty - row_const
        s_scr[pl.ds(start, blk), :] = s
        return jnp.maximum(m, jnp.max(s, axis=0, keepdims=True))

    def past(j, m):
        row_const = slope * ((qi - j) * blk).astype(F32) + sel_scr[pl.ds(j, 1), :]
        return scores(j, alibi, row_const, m)

    m = scores(qi, jnp.where(rel >= 0.0, alibi, -NEG_INF), 0.0, jnp.full((1, blk), NEG_INF, F32))
    m = _loop_tiles(0, qi, past, m)

    def accumulate(j, carry):
        l, acc = carry
        start = pl.multiple_of(j * blk, blk)
        p = jnp.exp(s_scr[pl.ds(start, blk), :] - m)
        vj = v_ref[:, pl.ds(start, blk)].astype(BF16)
        return (l + jnp.sum(p, axis=0, keepdims=True),
                acc + jnp.dot(vj, p.astype(BF16), preferred_element_type=F32))

    l, acc = _loop_tiles(0, qi + 1, accumulate, (jnp.zeros((1, blk), F32), jnp.zeros((HEAD_DIM, blk), F32)))
    o_ref[...] = acc / l


def _moba_prompt(yt, slopes, q_row0, k_row0, v_row0):
    bsz, _, seq = yt.shape
    assert seq % MOBA_BLOCK == 0
    n_blk = seq // MOBA_BLOCK
    hd = HEAD_DIM

    def kvspec(row0):
        return pl.BlockSpec((None, hd, seq), lambda b, h, i: (b, row0 // hd + h, 0))

    return pl.pallas_call(
        functools.partial(_moba_p_kernel, n_blk=n_blk),
        grid=(bsz, MOBA_HEADS, n_blk),
        in_specs=[
            pl.BlockSpec(memory_space=pltpu.SMEM),
            pl.BlockSpec((None, hd, MOBA_BLOCK), lambda b, h, i: (b, q_row0 // hd + h, i)),
            kvspec(k_row0),
            kvspec(v_row0),
        ],
        out_specs=pl.BlockSpec((None, hd, MOBA_BLOCK), lambda b, h, i: (b, h, i)),
        out_shape=jax.ShapeDtypeStruct((bsz, MOBA_HEADS * hd, seq), F32),
        scratch_shapes=[pltpu.VMEM((max(n_blk, SUBLANES), MOBA_BLOCK), F32), pltpu.VMEM((seq, MOBA_BLOCK), F32)],
        compiler_params=_cparams("parallel", "parallel", "arbitrary"),
        name="moba_prompt",
    )(slopes, yt, yt, yt)


def _log_sigmoid(x):
    return jnp.minimum(x, 0.0) - jnp.log1p(jnp.exp(-jnp.abs(x)))


def _gla_p_kernel(q_ref, k_ref, v_ref, ga_ref, rg_ref, gw_ref, gb_ref, on_ref, o_ref, s_ref, st_scr):
    ti = pl.program_id(2)
    ch = GLA_CHUNK
    n_ch = LANES // ch

    @pl.when(ti == 0)
    def _():
        st_scr[...] = jnp.zeros_like(st_scr)

    q = q_ref[...]
    k = k_ref[...]
    v = v_ref[...]
    ga = ga_ref[0:GLA_GATE_RANK, :]
    x = jnp.dot(gw_ref[...].astype(BF16), ga.astype(BF16), preferred_element_type=F32) + gb_ref[...]
    g = _log_sigmoid(x) / GLA_TAU

    lane_k = lax.broadcasted_iota(jnp.int32, (GLA_DK, LANES), 1)
    pos = lane_k % ch
    b = g
    c = g
    for sh in (1, 2, 4, 8):
        b = b + jnp.where(pos >= sh, pltpu.roll(b, sh, 1), 0.0)
        c = c + jnp.where(pos < ch - sh, pltpu.roll(c, LANES - sh, 1), 0.0)
    tot = b + c - g

    qd = q * jnp.exp(b)
    kd = k * jnp.exp(tot - b)

    o = jnp.zeros((GLA_DV, LANES), F32)
    for dl in range(ch):
        if dl == 0:
            kk, bb, vv = k, b, v
        else:
            kk, bb, vv = pltpu.roll(k, dl, 1), pltpu.roll(b, dl, 1), pltpu.roll(v, dl, 1)
        a = jnp.where(pos >= dl, q * kk * jnp.exp(b - bb), 0.0)
        o = o + jnp.sum(a, axis=0, keepdims=True) * vv

    cid_v = lax.broadcasted_iota(jnp.int32, (GLA_DV, LANES), 1) // ch
    cid_k = lane_k // ch
    vm = jnp.concatenate([jnp.where(cid_v == cc, v, 0.0) for cc in range(n_ch)], axis=0).astype(BF16)
    ut = lax.dot_general(vm, kd.astype(BF16), _NT, preferred_element_type=F32)
    dt = jnp.exp(tot).T
    st = st_scr[...]
    pieces = []
    for cc in range(n_ch):
        pieces.append(st.astype(BF16))
        st = st * dt[cc * ch:cc * ch + 1, :] + ut[cc * GLA_DV:(cc + 1) * GLA_DV, :]
    st_scr[...] = st
    sstack = jnp.concatenate(pieces, axis=1)
    qm = jnp.concatenate([jnp.where(cid_k == cc, qd, 0.0) for cc in range(n_ch)], axis=0).astype(BF16)
    o = o + jnp.dot(sstack, qm, preferred_element_type=F32)

    ms = jnp.mean(o * o, axis=0, keepdims=True)
    o_ref[...] = (o * lax.rsqrt(ms + RMS_EPS) * on_ref[...]) * _silu(rg_ref[...])

    @pl.when(ti == pl.num_programs(2) - 1)
    def _():
        s_ref[...] = st.T


def _gla_prompt(yt, rows, gw_t, gb, on):
    bsz, _, seq = yt.shape
    nt = seq // LANES
    q0, k0, v0, rg0, ga0 = rows

    def at(row0, size):
        return pl.BlockSpec((None, size, LANES), lambda b, h, t: (b, row0 // size + h, t))

    hspec = lambda size, last: pl.BlockSpec((None, size, last), lambda b, h, t: (h, 0, 0))
    return pl.pallas_call(
        _gla_p_kernel,
        grid=(bsz, GLA_HEADS, nt),
        in_specs=[
            at(q0, GLA_DK), at(k0, GLA_DK), at(v0, GLA_DV),
            pl.BlockSpec((None, LANES, LANES), lambda b, h, t: (b, ga0 // LANES, t)),
            at(rg0, GLA_DV),
            hspec(GLA_DK, GLA_GATE_RANK), hspec(GLA_DK, 1), hspec(GLA_DV, 1),
        ],
        out_specs=[
            pl.BlockSpec((None, GLA_DV, LANES), lambda b, h, t: (b, h, t)),
            pl.BlockSpec((None, None, GLA_DK, GLA_DV), lambda b, h, t: (b, h, 0, 0)),
        ],
        out_shape=[
            jax.ShapeDtypeStruct((bsz, GLA_HEADS * GLA_DV, seq), F32),
            jax.ShapeDtypeStruct((bsz, GLA_HEADS, GLA_DK, GLA_DV), F32),
        ],
        scratch_shapes=[pltpu.VMEM((GLA_DV, GLA_DK), F32)],
        compiler_params=_cparams("parallel", "parallel", "arbitrary"),
        name="gla_prompt",
    )(yt, yt, yt, yt, yt, gw_t, gb, on)


def _gla_s_kernel(q_ref, k_ref, v_ref, rg_ref, ga_ref, gw_ref, gb_ref, on_ref, s0_ref, o_ref, s_ref):
    x = jnp.sum(gw_ref[...] * ga_ref[...], axis=-1, keepdims=True) + gb_ref[...]
    a = jnp.exp(_log_sigmoid(x) / GLA_TAU)
    s = a * s0_ref[...] + k_ref[...] * v_ref[...]
    s_ref[...] = s
    o = jnp.sum(q_ref[...] * s, axis=1, keepdims=True)
    ms = jnp.mean(o * o, axis=-1, keepdims=True)
    o_ref[...] = (o * lax.rsqrt(ms + RMS_EPS) * on_ref[...]) * _silu(rg_ref[...])


def _gla_sample(q, k, v, rg, ga, gw_t, gb, on, s0):
    bsz = q.shape[0]
    per_b = lambda a: pl.BlockSpec((None,) + a.shape[1:], lambda b: (b, 0, 0, 0))
    full = lambda a: pl.BlockSpec(a.shape, lambda b: (0,) * a.ndim)
    return pl.pallas_call(
        _gla_s_kernel,
        grid=(bsz,),
        in_specs=[per_b(q), per_b(k), per_b(v), per_b(rg), per_b(ga), full(gw_t), full(gb), full(on), per_b(s0)],
        out_specs=[per_b(v), per_b(s0)],
        out_shape=[jax.ShapeDtypeStruct(v.shape, F32), jax.ShapeDtypeStruct(s0.shape, F32)],
        compiler_params=_cparams("parallel"),
        name="gla_sample",
    )(q, k, v, rg, ga, gw_t, gb, on, s0)


MOBA_PAGES_PER_STEP = 16


def _moba_gate_kernel(pt_ref, q_ref, *refs, n_blk):
    pages, (gate_ref, top_ref) = refs[:MOBA_PAGES_PER_STEP], refs[MOBA_PAGES_PER_STEP:]
    s = pl.program_id(1)
    per_step = MOBA_PAGES_PER_STEP // 2
    lane = lax.broadcasted_iota(jnp.int32, (MOBA_HEADS, LANES), 1)

    @pl.when(s == 0)
    def _():
        gate_ref[...] = jnp.full_like(gate_ref, NEG_INF)

    q = _r16(q_ref[...])
    gate = gate_ref[...]
    for u in range(per_step):
        ksum = pages[2 * u][...] + pages[2 * u + 1][...]
        kmean = jnp.sum(ksum, axis=-1, keepdims=True) / float(MOBA_BLOCK)
        gv = jnp.sum(q * _r16(kmean), axis=1)
        gate = jnp.where(lane == s * per_step + u, gv, gate)
    gate_ref[...] = gate

    @pl.when(s == pl.num_programs(1) - 1)
    def _():
        g = gate
        top = jnp.zeros((MOBA_HEADS, LANES), jnp.int32)
        for r in range(MOBA_TOPK):
            m = jnp.max(g, axis=-1, keepdims=True)
            idx = jnp.min(jnp.where(g == m, lane, LANES), axis=-1, keepdims=True)
            top = jnp.where(lane == r, idx, top)
            g = jnp.where(lane == idx, -jnp.inf, g)
        top_ref[...] = top


def _moba_gates(page_table, q, kpool_t):
    bsz, n_pages = page_table.shape
    n_blk = n_pages // 2
    assert MOBA_TOPK <= n_blk <= LANES and n_pages % MOBA_PAGES_PER_STEP == 0
    pps = MOBA_PAGES_PER_STEP

    def page_spec(u):
        return pl.BlockSpec((None, MOBA_HEADS, HEAD_DIM, LANES), lambda b, s, pt: (pt[b, s * pps + u], 0, 0, 0))

    out_spec = pl.BlockSpec((None, MOBA_HEADS, LANES), lambda b, s, pt: (b, 0, 0))
    return pl.pallas_call(
        functools.partial(_moba_gate_kernel, n_blk=n_blk),
        grid_spec=pltpu.PrefetchScalarGridSpec(
            num_scalar_prefetch=1,
            grid=(bsz, n_pages // pps),
            in_specs=[pl.BlockSpec((None, MOBA_HEADS, HEAD_DIM, 1), lambda b, s, pt: (b, 0, 0, 0))]
            + [page_spec(u) for u in range(pps)],
            out_specs=[out_spec, out_spec],
        ),
        out_shape=[jax.ShapeDtypeStruct((bsz, MOBA_HEADS, LANES), F32),
                   jax.ShapeDtypeStruct((bsz, MOBA_HEADS, LANES), jnp.int32)],
        compiler_params=_cparams("parallel", "arbitrary"),
        name="moba_gates",
    )(page_table, q, *([kpool_t] * pps))


def _moba_s_kernel(pt_ref, top_ref, slopes_ref, q_ref, kn_ref, vn_ref, *refs, past):
    n_pg = 2 * MOBA_TOPK
    k_refs, v_refs, o_ref = refs[:n_pg], refs[n_pg:2 * n_pg], refs[2 * n_pg]
    b = pl.program_id(0)
    h = pl.program_id(1)
    slope = slopes_ref[h]
    q = _r16(q_ref[...])
    lane = lax.broadcasted_iota(jnp.int32, (1, LANES), 1)
    scores = []
    for r in range(MOBA_TOPK):
        for half in range(2):
            kpos = top_ref[b, h * MOBA_TOPK + r] * MOBA_BLOCK + half * LANES + lane
            s = jnp.sum(_r16(k_refs[2 * r + half][...]) * q, axis=0, keepdims=True)
            scores.append(s - slope * (past - kpos).astype(F32))
    s_new = jnp.sum(_r16(kn_ref[...]) * q, axis=0, keepdims=True)
    m = s_new
    for s in scores:
        m = jnp.maximum(m, jnp.max(s, axis=-1, keepdims=True))
    p_new = jnp.exp(s_new - m)
    probs = [jnp.exp(s - m) for s in scores]
    l = p_new
    for p in probs:
        l = l + jnp.sum(p, axis=-1, keepdims=True)
    o = _r16(p_new / l) * _r16(vn_ref[...])
    for p, v_ref in zip(probs, v_refs):
        o = o + jnp.sum(_r16(v_ref[...]) * _r16(p / l), axis=-1, keepdims=True)
    o_ref[...] = o


def _moba_sample(page_table, top, slopes, q, k_new, v_new, kpool_t, vpool_t):
    bsz, n_pages = page_table.shape
    past = n_pages * LANES

    def col_spec():
        return pl.BlockSpec((None, None, HEAD_DIM, 1), lambda b, h, pt, tp: (b, h, 0, 0))

    def page_spec(r, half):
        return pl.BlockSpec((None, None, HEAD_DIM, LANES),
                            lambda b, h, pt, tp: (pt[b, 2 * tp[b, h * MOBA_TOPK + r] + half], h, 0, 0))

    pages = [page_spec(r, half) for r in range(MOBA_TOPK) for half in range(2)]
    return pl.pallas_call(
        functools.partial(_moba_s_kernel, past=past),
        grid_spec=pltpu.PrefetchScalarGridSpec(
            num_scalar_prefetch=2,
            grid=(bsz, MOBA_HEADS),
            in_specs=[pl.BlockSpec(memory_space=pltpu.SMEM), col_spec(), col_spec(), col_spec()] + pages + pages,
            out_specs=col_spec(),
        ),
        out_shape=jax.ShapeDtypeStruct((bsz, MOBA_HEADS, HEAD_DIM, 1), F32),
        compiler_params=_cparams("parallel", "parallel"),
        name="moba_sample",
    )(page_table, top, slopes, q, k_new, v_new, *([kpool_t] * len(pages)), *([vpool_t] * len(pages)))


def _cmp_stage1_kernel(*refs, n_in):
    x_refs, w_ref, o_ref, x_scr = refs[:n_in], refs[n_in], refs[n_in + 1], refs[n_in + 2]
    st = NSA_CMP_STRIDE
    row = 0
    for r in x_refs:
        w = r.shape[-1]
        x_scr[row:row + w, :] = r[...].T
        row += w
    n_chunk = row // st
    xcat = jnp.concatenate([x_scr[pl.ds(s, n_chunk, stride=st), :] for s in range(st)], axis=1)
    o_ref[...] = jnp.dot(xcat.astype(BF16), w_ref[...], preferred_element_type=F32)


def _cmp_w1cat(w1):
    ratio = NSA_CMP_LEN // NSA_CMP_STRIDE
    w = w1.reshape(ratio, NSA_CMP_STRIDE * HEAD_DIM, NSA_CMP_HIDDEN)
    return jnp.concatenate([w[r] for r in range(ratio)], axis=1).astype(BF16)


def _cmp_stage1_dense(xt, row0, w1cat):
    bsz, _, seq = xt.shape
    n_chunk = seq // NSA_CMP_STRIDE
    return pl.pallas_call(
        functools.partial(_cmp_stage1_kernel, n_in=1),
        grid=(bsz, NSA_KV_HEADS),
        in_specs=[pl.BlockSpec((None, HEAD_DIM, seq), lambda b, g: (b, row0 // HEAD_DIM + g, 0)),
                  pl.BlockSpec(w1cat.shape, lambda b, g: (0, 0))],
        out_specs=pl.BlockSpec((None, None, n_chunk, LANES), lambda b, g: (b, g, 0, 0)),
        out_shape=jax.ShapeDtypeStruct((bsz, NSA_KV_HEADS, n_chunk, LANES), F32),
        scratch_shapes=[pltpu.VMEM((seq, HEAD_DIM), F32)],
        compiler_params=_cparams("parallel", "parallel"),
        name="cmp_stage1_dense",
    )(xt, w1cat)


CMP_PAGES_PER_STEP = 16


def _cmp_stage1_paged_kernel(pt_ref, *refs):
    pps, ng, st = CMP_PAGES_PER_STEP, NSA_KV_HEADS, NSA_CMP_STRIDE
    x_refs, w_ref, o_ref, x_scr = refs[:pps], refs[pps], refs[pps + 1], refs[pps + 2]
    for gp in range(ng // 2):
        for u, r in enumerate(x_refs):
            row = (gp * pps + u) * LANES
            x_scr[row:row + LANES, :] = jnp.concatenate([r[2 * gp], r[2 * gp + 1]], axis=0).T
    n_rows = (ng // 2) * pps * LANES // st
    xcat = jnp.concatenate([x_scr[pl.ds(s, n_rows, stride=st), :] for s in range(st)], axis=1)
    y = jnp.dot(xcat.astype(BF16), w_ref[...], preferred_element_type=F32)
    cps = pps * LANES // st
    for gp in range(ng // 2):
        for gl in range(2):
            o_ref[2 * gp + gl] = y[gp * cps:(gp + 1) * cps, gl * LANES:(gl + 1) * LANES]


def _cmp_stage1_paged(page_table, pool_t, w1cat):
    bsz, n_pages = page_table.shape
    pps = CMP_PAGES_PER_STEP
    assert n_pages % pps == 0 and NSA_KV_HEADS % 2 == 0
    cps = pps * LANES // NSA_CMP_STRIDE
    n_chunk = n_pages * LANES // NSA_CMP_STRIDE
    w3 = w1cat.reshape(NSA_CMP_STRIDE, 1, HEAD_DIM, 1, LANES)
    eye = jnp.eye(2, dtype=w1cat.dtype).reshape(1, 2, 1, 2, 1)
    w1cat = (w3 * eye).reshape(NSA_CMP_STRIDE * 2 * HEAD_DIM, 2 * LANES)

    def page_spec(u):
        return pl.BlockSpec((None, NSA_KV_HEADS, HEAD_DIM, LANES), lambda b, s, pt: (pt[b, s * pps + u], 0, 0, 0))

    return pl.pallas_call(
        _cmp_stage1_paged_kernel,
        grid_spec=pltpu.PrefetchScalarGridSpec(
            num_scalar_prefetch=1,
            grid=(bsz, n_pages // pps),
            in_specs=[page_spec(u) for u in range(pps)] + [pl.BlockSpec(w1cat.shape, lambda b, s, pt: (0, 0))],
            out_specs=pl.BlockSpec((None, NSA_KV_HEADS, cps, LANES), lambda b, s, pt: (b, 0, s, 0)),
            scratch_shapes=[pltpu.VMEM((NSA_KV_HEADS // 2 * pps * LANES, LANES), F32)],
        ),
        out_shape=jax.ShapeDtypeStruct((bsz, NSA_KV_HEADS, n_chunk, LANES), F32),
        compiler_params=_cparams("parallel", "arbitrary"),
        name="cmp_stage1_paged",
    )(page_table, *([pool_t] * pps), w1cat)


def _cmp_stage2_kernel(p_ref, pe_ref, w1_ref, b1_ref, w2_ref, b2_ref, gain_ref, o_ref, *, normalize):
    p = p_ref[...]
    n = p.shape[0]
    const = jnp.dot(pe_ref[...].astype(BF16), w1_ref[...].astype(BF16), preferred_element_type=F32) + b1_ref[...]
    shifted = pltpu.roll(pltpu.roll(p, n - 1, 0), NSA_CMP_HIDDEN, 1)
    hid = (p + shifted)[:, :NSA_CMP_HIDDEN] + const
    out = jnp.dot(_silu(hid).astype(BF16), w2_ref[...], preferred_element_type=F32) + b2_ref[...]
    if normalize:
        ms = jnp.mean(out * out, axis=-1, keepdims=True)
        out = out * lax.rsqrt(ms + RMS_EPS) * gain_ref[...]
    o_ref[...] = out.T


def _cmp_stage2(parts, pe, w1, b1, w2, b2, gain):
    bsz, ng, n_chunk, _ = parts.shape
    normalize = gain is not None
    gain = jnp.ones((HEAD_DIM,), F32) if gain is None else gain
    full = lambda a: pl.BlockSpec(a.shape, lambda b, g: (0,) * a.ndim)
    args = (pe.reshape(1, -1), w1, b1.reshape(1, -1), w2.astype(BF16), b2.reshape(1, -1), gain.reshape(1, -1))
    return pl.pallas_call(
        functools.partial(_cmp_stage2_kernel, normalize=normalize),
        grid=(bsz, ng),
        in_specs=[pl.BlockSpec((None, None, n_chunk, LANES), lambda b, g: (b, g, 0, 0))] + [full(a) for a in args],
        out_specs=pl.BlockSpec((None, None, HEAD_DIM, n_chunk), lambda b, g: (b, g, 0, 0)),
        out_shape=jax.ShapeDtypeStruct((bsz, ng, HEAD_DIM, n_chunk), F32),
        compiler_params=_cparams("parallel", "parallel"),
        name="cmp_stage2",
    )(parts, *args)


NSA_TQ = 128


def _nsa_p_kernel(slopes_ref, q_ref, kc_ref, vc_ref, ks_ref, vs_ref, kw_ref, vw_ref, gt_ref, gb_ref, ov_ref,
                  o_ref, sel_scr, s_slc, s_win, *, n_cmp, n_blk):
    g = pl.program_id(1)
    qi = pl.program_id(2)
    tq, grp, hd = NSA_TQ, NSA_GROUP, HEAD_DIM
    wq = grp * tq
    q4 = jnp.concatenate([q_ref[r * hd:(r + 1) * hd, :] for r in range(grp)], axis=1)
    qb = q4.astype(BF16)
    lane = lax.broadcasted_iota(jnp.int32, (1, wq), 1)
    qpos = qi * tq + lane % tq
    slope = jnp.zeros((1, wq), F32)
    for r in range(grp):
        slope = jnp.where(lane // tq == r, slopes_ref[g * grp + r], slope)

    def softmax_block(kt, vt, kpos, mask, carry):
        m, l, acc = carry
        s = lax.dot_general(kt, qb, _TN, preferred_element_type=F32)
        s = jnp.where(mask, s - slope * (qpos - kpos).astype(F32), NEG_INF)
        m_new = jnp.maximum(m, jnp.max(s, axis=0, keepdims=True))
        p = jnp.where(mask, jnp.exp(s - m_new), 0.0)
        alpha = jnp.exp(m - m_new)
        l = alpha * l + jnp.sum(p, axis=0, keepdims=True)
        acc = alpha * acc + jnp.dot(vt, p.astype(BF16), preferred_element_type=F32)
        return (m_new, l, acc), p

    def finish(carry):
        _, l, acc = carry
        return acc / jnp.where(l > 0.0, l, 1.0)

    init = (jnp.full((1, wq), NEG_INF, F32), jnp.zeros((1, wq), F32), jnp.zeros((hd, wq), F32))

    nc = kc_ref.shape[-1]
    ci = lax.broadcasted_iota(jnp.int32, (nc, 1), 0)
    c_end = ci * NSA_CMP_STRIDE + (NSA_CMP_LEN - 1)
    mask_c = (c_end <= qpos) & (ci < n_cmp)
    carry_c, p_c = softmax_block(kc_ref[...].astype(BF16), vc_ref[...].astype(BF16), c_end, mask_c, init)
    o_c = finish(carry_c)
    prob_c = p_c / jnp.where(carry_c[1] > 0.0, carry_c[1], 1.0)

    imp4 = jnp.dot(ov_ref[...].astype(BF16), prob_c.astype(BF16), preferred_element_type=F32)
    imp = imp4[:, 0:tq]
    for r in range(1, grp):
        imp = imp + imp4[:, r * tq:(r + 1) * tq]
    bi = lax.broadcasted_iota(jnp.int32, (n_blk, tq), 0)
    cur = (qi * tq + lax.broadcasted_iota(jnp.int32, (n_blk, tq), 1)) // NSA_SLC_BLOCK
    allowed = bi <= cur
    forced = (bi < NSA_INIT_BLOCKS) | (cur - bi < NSA_LOCAL_BLOCKS)
    val = jnp.where(allowed & forced, FORCE_SCORE, jnp.where(allowed, imp, NEG_INF))
    rank = jnp.zeros((n_blk, tq), F32)
    for j in range(n_blk):
        vj = val[j:j + 1, :]
        beats = (vj > val) | ((vj == val) & (j < bi))
        rank = rank + jnp.where(beats, 1.0, 0.0)
    big = -NEG_INF
    pen = jnp.where((rank < float(NSA_SLC_TOPN)) & allowed, 0.0, big)
    sel_scr[0:n_blk, :] = jnp.concatenate([pen] * grp, axis=1)

    half = NSA_SLC_BLOCK
    ksub = lax.broadcasted_iota(jnp.int32, (tq, 1), 0)
    rel = (lane % tq - ksub).astype(F32)
    alibi = slope * rel
    pen_diag = jnp.where(rel >= 0.0, alibi, big)
    pen_low = jnp.where(rel <= 0.0, alibi, big)

    def scores(k_ref_, scr, jj, row, penalty, parts, m):
        start = pl.multiple_of(jj * tq, tq)
        kt = k_ref_[:, pl.ds(start, tq)].astype(BF16)
        s = lax.dot_general(kt, qb, _TN, preferred_element_type=F32) - penalty
        for r0, r1, c in parts:
            sp = s[r0:r1] - c
            scr[pl.ds(pl.multiple_of(row + r0, half), r1 - r0), :] = sp
            m = jnp.maximum(m, jnp.max(sp, axis=0, keepdims=True))
        return m

    def accumulate(v_ref_, scr, jj, row, m, carry):
        l, acc = carry
        start = pl.multiple_of(jj * tq, tq)
        p = jnp.exp(scr[pl.ds(pl.multiple_of(row, tq), tq), :] - m)
        vt = v_ref_[:, pl.ds(start, tq)].astype(BF16)
        return (l + jnp.sum(p, axis=0, keepdims=True),
                acc + jnp.dot(vt, p.astype(BF16), preferred_element_type=F32))

    def slc_parts(jj, row_const):
        return [(0, half, row_const + sel_scr[pl.ds(2 * jj, 1), :]),
                (half, tq, row_const + sel_scr[pl.ds(2 * jj + 1, 1), :])]

    def tile_const(jj):
        return slope * ((qi - jj) * tq).astype(F32)

    m0 = jnp.full((1, wq), NEG_INF, F32)
    la0 = (jnp.zeros((1, wq), F32), jnp.zeros((hd, wq), F32))

    m = scores(ks_ref, s_slc, qi, qi * tq, pen_diag, slc_parts(qi, 0.0), m0)
    m = _loop_tiles(
        0, qi, lambda jj, mm: scores(ks_ref, s_slc, jj, jj * tq, alibi, slc_parts(jj, tile_const(jj)), mm), m)
    l, acc = _loop_tiles(0, qi + 1, lambda jj, c: accumulate(vs_ref, s_slc, jj, jj * tq, m, c), la0)
    o_s = acc / l

    n_wt = NSA_WINDOW // tq
    wrow = lambda jj: (jj - (qi - n_wt)) * tq
    first = jnp.maximum(qi - (n_wt - 1), 0)
    m = scores(kw_ref, s_win, qi, wrow(qi), pen_diag, [(0, tq, 0.0)], m0)
    m = _loop_tiles(
        first, qi, lambda jj, mm: scores(kw_ref, s_win, jj, wrow(jj), alibi, [(0, tq, tile_const(jj))], mm), m)
    has_low = jnp.where(qi >= n_wt, 1, 0)
    m = lax.fori_loop(
        0, has_low,
        lambda _, mm: scores(kw_ref, s_win, qi - n_wt, wrow(qi - n_wt), pen_low,
                             [(0, tq, tile_const(qi - n_wt))], mm), m)
    l, acc = _loop_tiles(first - has_low, qi + 1, lambda jj, c: accumulate(vw_ref, s_win, jj, wrow(jj), m, c), la0)
    o_w = acc / l

    outs = []
    for r in range(grp):
        acc = jnp.zeros((hd, tq), F32)
        for br, o_b in enumerate((o_c, o_s, o_w)):
            row = (g * grp + r) * 3 + br
            gt = gt_ref[pl.ds(row, 1), :] + gb_ref[pl.ds(row, 1), :]
            acc = acc + (1.0 / (1.0 + jnp.exp(-gt))) * o_b[:, r * tq:(r + 1) * tq]
        outs.append(acc)
    o_ref[...] = jnp.concatenate(outs, axis=0)


def _nsa_overlap_t(n_cmp_pad, n_blk):
    c_start = np.arange(n_cmp_pad) * NSA_CMP_STRIDE
    c_end = c_start + NSA_CMP_LEN - 1
    b_start = np.arange(n_blk) * NSA_SLC_BLOCK
    ov = (c_start[None, :] < b_start[:, None] + NSA_SLC_BLOCK) & (c_end[None, :] >= b_start[:, None])
    return jnp.asarray(ov.astype(np.float32))


def _nsa_prompt(yt, kcmp_t, vcmp_t, slopes, rows, gate_b_col):
    bsz, _, seq = yt.shape
    q0, ks0, vs0, kw0, vw0, g0 = rows
    n_chunk = kcmp_t.shape[-1]
    n_cmp = max((seq - NSA_CMP_LEN) // NSA_CMP_STRIDE + 1, 1)
    n_blk = max(-(-seq // NSA_SLC_BLOCK), NSA_SLC_TOPN)
    assert seq % NSA_TQ == 0 and n_blk * NSA_SLC_BLOCK == seq and n_blk % SUBLANES == 0
    assert NSA_TQ == 2 * NSA_SLC_BLOCK and NSA_WINDOW % NSA_TQ == 0 and NSA_LOCAL_BLOCKS >= 1
    hd, wq = HEAD_DIM, NSA_GROUP * NSA_TQ
    ov = _nsa_overlap_t(n_chunk, n_blk)

    def kv(row0):
        return pl.BlockSpec((None, hd, seq), lambda b, g, i: (b, row0 // hd + g, 0))

    cmp_spec = pl.BlockSpec((None, None, hd, n_chunk), lambda b, g, i: (b, g, 0, 0))
    return pl.pallas_call(
        functools.partial(_nsa_p_kernel, n_cmp=n_cmp, n_blk=n_blk),
        grid=(bsz, NSA_KV_HEADS, seq // NSA_TQ),
        in_specs=[
            pl.BlockSpec(memory_space=pltpu.SMEM),
            pl.BlockSpec((None, NSA_GROUP * hd, NSA_TQ), lambda b, g, i: (b, q0 // (NSA_GROUP * hd) + g, i)),
            cmp_spec, cmp_spec, kv(ks0), kv(vs0), kv(kw0), kv(vw0),
            pl.BlockSpec((None, LANES, NSA_TQ), lambda b, g, i: (b, g0 // LANES, i)),
            pl.BlockSpec((LANES, 1), lambda b, g, i: (0, 0)),
            pl.BlockSpec(ov.shape, lambda b, g, i: (0, 0)),
        ],
        out_specs=pl.BlockSpec((None, NSA_GROUP * hd, NSA_TQ), lambda b, g, i: (b, g, i)),
        out_shape=jax.ShapeDtypeStruct((bsz, NSA_HEADS * hd, seq), F32),
        scratch_shapes=[pltpu.VMEM((n_blk, wq), F32), pltpu.VMEM((seq, wq), F32),
                        pltpu.VMEM((NSA_WINDOW + NSA_TQ, wq), F32)],
        compiler_params=_cparams("parallel", "parallel", "arbitrary"),
        name="nsa_prompt",
    )(slopes, yt, kcmp_t, vcmp_t, yt, yt, yt, yt, yt, gate_b_col, ov)


def _nsa_s_cmp_kernel(slopes_ref, q_ref, kc_ref, vc_ref, ov_ref, oc_ref, sel_ref, *, past, n_cmp, n_blk):
    g = pl.program_id(1)
    grp = NSA_GROUP
    nc = kc_ref.shape[-1]
    nb = ov_ref.shape[0]
    ci = lax.broadcasted_iota(jnp.int32, (1, nc), 1)
    c_end = ci * NSA_CMP_STRIDE + (NSA_CMP_LEN - 1)
    mask = (c_end <= past) & (ci < n_cmp)
    dist = (past - c_end).astype(F32)
    kc = _r16(kc_ref[...])
    vc = _r16(vc_ref[...])
    probs = []
    for r in range(grp):
        s = jnp.sum(kc * _r16(q_ref[r]), axis=0, keepdims=True)
        s = jnp.where(mask, s - slopes_ref[g * grp + r] * dist, NEG_INF)
        m = jnp.max(s, axis=-1, keepdims=True)
        p = jnp.where(mask, jnp.exp(s - m), 0.0)
        l = jnp.sum(p, axis=-1, keepdims=True)
        p = _r16(p / jnp.where(l > 0.0, l, 1.0))
        probs.append(p)
        oc_ref[r] = jnp.sum(vc * p, axis=-1, keepdims=True)
    p8 = jnp.concatenate(probs + [jnp.zeros((SUBLANES - grp, nc), F32)], axis=0).astype(BF16)
    imp = jnp.sum(lax.dot_general(p8, ov_ref[...].astype(BF16), _NT, preferred_element_type=F32),
                  axis=0, keepdims=True)

    bi = lax.broadcasted_iota(jnp.int32, (1, nb), 1)
    cur = past // NSA_SLC_BLOCK
    allowed = (bi <= cur) & (bi < n_blk)
    forced = (bi < NSA_INIT_BLOCKS) | (cur - bi < NSA_LOCAL_BLOCKS)
    val = jnp.where(allowed & forced, FORCE_SCORE, jnp.where(allowed, imp, NEG_INF))
    val = jnp.where(bi < n_blk, val, -jnp.inf)
    bj = lax.broadcasted_iota(jnp.int32, (nb, 1), 0)
    val_col = jnp.sum(jnp.where(bj == bi, val, 0.0), axis=-1, keepdims=True)
    beats = (val_col > val) | ((val_col == val) & (bj < bi))
    rank = jnp.sum(jnp.where(beats, 1.0, 0.0), axis=0, keepdims=True)
    lane = lax.broadcasted_iota(jnp.int32, (1, LANES), 1)
    out = jnp.full((1, LANES), -1, jnp.int32)
    for k in range(NSA_SLC_TOPN):
        hit = (rank == float(k)) & allowed
        idx = jnp.sum(jnp.where(hit, bi, 0), axis=-1, keepdims=True)
        cnt = jnp.sum(jnp.where(hit, 1, 0), axis=-1, keepdims=True)
        out = jnp.where(lane == k, jnp.where(cnt > 0, idx, -1), out)
    sel_ref[...] = out


def _nsa_sample_cmp(q, kcmp_t, vcmp_t, slopes, past, n_keys):
    bsz = q.shape[0]
    n_chunk = kcmp_t.shape[-1]
    n_cmp = max((n_keys - NSA_CMP_LEN) // NSA_CMP_STRIDE + 1, 1)
    n_blk = max(-(-n_keys // NSA_SLC_BLOCK), NSA_SLC_TOPN)
    nb_pad = -(-n_blk // LANES) * LANES
    ov = _nsa_overlap_t(n_chunk, nb_pad)
    qspec = pl.BlockSpec((None, None, NSA_GROUP, HEAD_DIM, 1), lambda b, g: (b, g, 0, 0, 0))
    cspec = pl.BlockSpec((None, None, HEAD_DIM, n_chunk), lambda b, g: (b, g, 0, 0))
    return pl.pallas_call(
        functools.partial(_nsa_s_cmp_kernel, past=past, n_cmp=n_cmp, n_blk=n_blk),
        grid=(bsz, NSA_KV_HEADS),
        in_specs=[pl.BlockSpec(memory_space=pltpu.SMEM), qspec, cspec, cspec,
                  pl.BlockSpec(ov.shape, lambda b, g: (0, 0))],
        out_specs=[qspec, pl.BlockSpec((None, None, 1, LANES), lambda b, g: (b, g, 0, 0))],
        out_shape=[jax.ShapeDtypeStruct(q.shape, F32),
                   jax.ShapeDtypeStruct((bsz, NSA_KV_HEADS, 1, LANES), jnp.int32)],
        compiler_params=_cparams("parallel", "parallel"),
        name="nsa_sample_cmp",
    )(slopes, q, kcmp_t, vcmp_t, ov)


def _nsa_s_attn_kernel(pt_ref, sel_ref, slopes_ref, q_ref, oc_ref, gt_ref, gb_ref, ksn_ref, vsn_ref, kwn_ref,
                       vwn_ref, kw_ref, vw_ref, *refs, past):
    n_sel = NSA_SLC_TOPN
    k_refs, v_refs, o_ref = refs[:n_sel], refs[n_sel:2 * n_sel], refs[2 * n_sel]
    b = pl.program_id(0)
    g = pl.program_id(1)
    grp = NSA_GROUP
    lane = lax.broadcasted_iota(jnp.int32, (1, LANES), 1)
    cur = past // NSA_SLC_BLOCK
    n_win = kw_ref.shape[-1]
    wpos = past - n_win + lax.broadcasted_iota(jnp.int32, (1, n_win), 1)
    wdist = past - wpos
    wmask = (wpos >= 0) & (wdist <= NSA_WINDOW)
    blks = [sel_ref[b, g * n_sel + k] for k in range(n_sel)]
    has_new = jnp.zeros((), jnp.int32)
    for blk in blks:
        has_new = jnp.maximum(has_new, (blk == cur).astype(jnp.int32))
    new_ok = jnp.full((1, 1), has_new, jnp.int32) > 0
    kb = [_r16(k_refs[k][...]) for k in range(n_sel)]
    vb = [_r16(v_refs[k][...]) for k in range(n_sel)]
    kwb = _r16(kw_ref[...])
    vwb = _r16(vw_ref[...])

    for r in range(grp):
        q = _r16(q_ref[r])
        slope = slopes_ref[g * grp + r]
        s_new = jnp.sum(_r16(ksn_ref[...]) * q, axis=0, keepdims=True)
        m = jnp.where(new_ok, s_new, NEG_INF)
        scores = []
        for k in range(n_sel):
            blk = blks[k]
            kpos = (blk // 2) * LANES + lane
            valid = (kpos // NSA_SLC_BLOCK == blk) & (blk >= 0) & (kpos < past)
            s = jnp.sum(kb[k] * q, axis=0, keepdims=True)
            s = jnp.where(valid, s - slope * (past - kpos).astype(F32), NEG_INF)
            scores.append((s, valid))
            m = jnp.maximum(m, jnp.max(s, axis=-1, keepdims=True))
        p_new = jnp.where(new_ok, jnp.exp(s_new - m), 0.0)
        probs = [jnp.where(valid, jnp.exp(s - m), 0.0) for s, valid in scores]
        l = p_new
        for p in probs:
            l = l + jnp.sum(p, axis=-1, keepdims=True)
        inv_l = 1.0 / jnp.where(l > 0.0, l, 1.0)
        o_s = _r16(p_new * inv_l) * _r16(vsn_ref[...])
        for p, v in zip(probs, vb):
            o_s = o_s + jnp.sum(v * _r16(p * inv_l), axis=-1, keepdims=True)
        sw = jnp.sum(kwb * q, axis=0, keepdims=True)
        sw = jnp.where(wmask, sw - slope * wdist.astype(F32), NEG_INF)
        sw_new = jnp.sum(_r16(kwn_ref[...]) * q, axis=0, keepdims=True)
        mw = jnp.maximum(jnp.max(sw, axis=-1, keepdims=True), sw_new)
        pw = jnp.where(wmask, jnp.exp(sw - mw), 0.0)
        pw_new = jnp.exp(sw_new - mw)
        inv_lw = 1.0 / (jnp.sum(pw, axis=-1, keepdims=True) + pw_new)
        o_w = (jnp.sum(vwb * _r16(pw * inv_lw), axis=-1, keepdims=True)
               + _r16(pw_new * inv_lw) * _r16(vwn_ref[...]))
        gates = [1.0 / (1.0 + jnp.exp(-(gt_ref[r * 3 + br:r * 3 + br + 1, :] + gb_ref[r * 3 + br:r * 3 + br + 1, :])))
                 for br in range(3)]
        o_ref[r] = gates[0] * oc_ref[r] + gates[1] * o_s + gates[2] * o_w


def _nsa_sample_attn(page_table, sel, slopes, q, o_c, gt, gb, new_rows, kwin_t, vwin_t, kpool_t, vpool_t):
    bsz, n_pages = page_table.shape
    past = n_pages * LANES
    assert past % LANES == 0
    n_sel = NSA_SLC_TOPN
    n_win = kwin_t.shape[-1]

    def im(f):
        return lambda b, g, pt, sl: f(b, g, pt, sl)

    qspec = pl.BlockSpec((None, None, NSA_GROUP, HEAD_DIM, 1), lambda b, g, pt, sl: (b, g, 0, 0, 0))
    col = pl.BlockSpec((None, None, HEAD_DIM, 1), lambda b, g, pt, sl: (b, g, 0, 0))
    win = pl.BlockSpec((None, None, HEAD_DIM, n_win), lambda b, g, pt, sl: (b, g, 0, 0))

    def page_spec(k):
        def index(b, g, pt, sl):
            page = jnp.clip(sl[b, g * n_sel + k] // 2, 0, n_pages - 1)
            return (pt[b, page], g, 0, 0)
        return pl.BlockSpec((None, None, HEAD_DIM, LANES), index)

    pages = [page_spec(k) for k in range(n_sel)]
    return pl.pallas_call(
        functools.partial(_nsa_s_attn_kernel, past=past),
        grid_spec=pltpu.PrefetchScalarGridSpec(
            num_scalar_prefetch=2,
            grid=(bsz, NSA_KV_HEADS),
            in_specs=[pl.BlockSpec(memory_space=pltpu.SMEM), qspec, qspec,
                      pl.BlockSpec((None, None, 3 * NSA_GROUP, 1), lambda b, g, pt, sl: (b, g, 0, 0)),
                      pl.BlockSpec((None, 3 * NSA_GROUP, 1), lambda b, g, pt, sl: (g, 0, 0)),
                      col, col, col, col, win, win] + pages + pages,
            out_specs=qspec,
        ),
        out_shape=jax.ShapeDtypeStruct(q.shape, F32),
        compiler_params=_cparams("parallel", "parallel"),
        name="nsa_sample_attn",
    )(page_table, sel, slopes, q, o_c, gt, gb, *new_rows, kwin_t, vwin_t,
      *([kpool_t] * n_sel), *([vpool_t] * n_sel))


MOE_TOKEN_TILE = 256
MOE_ROW_BLOCK = 512


def _router_kernel(x_ref, g_ref, rt_ref, h_ref, idx_ref, gate_ref):
    x = x_ref[...]
    ms = jnp.mean(x * x, axis=-1, keepdims=True)
    h = x * lax.rsqrt(ms + RMS_EPS) * g_ref[...]
    h_ref[...] = h
    logits = lax.dot_general(rt_ref[...].astype(BF16), h.astype(BF16), _NT, preferred_element_type=F32)
    e_id = lax.broadcasted_iota(jnp.int32, logits.shape, 0)
    v1 = jnp.max(logits, axis=0, keepdims=True)
    i1 = jnp.min(jnp.where(logits == v1, e_id, N_EXPERTS), axis=0, keepdims=True)
    rest = jnp.where(e_id == i1, -jnp.inf, logits)
    v2 = jnp.max(rest, axis=0, keepdims=True)
    i2 = jnp.min(jnp.where(rest == v2, e_id, N_EXPERTS), axis=0, keepdims=True)
    e2 = jnp.exp(v2 - v1)
    idx_ref[...] = jnp.concatenate([i1, i2], axis=0)
    gate_ref[...] = jnp.concatenate([1.0 / (1.0 + e2), e2 / (1.0 + e2)], axis=0)


def _router(x, g, router_t):
    n, d = x.shape
    tm = MOE_TOKEN_TILE
    return pl.pallas_call(
        _router_kernel,
        grid=(n // tm,),
        in_specs=[pl.BlockSpec((tm, d), lambda i: (i, 0)), pl.BlockSpec((1, d), lambda i: (0, 0)),
                  pl.BlockSpec(router_t.shape, lambda i: (0, 0))],
        out_specs=[pl.BlockSpec((tm, d), lambda i: (i, 0)), pl.BlockSpec((TOP_K, tm), lambda i: (0, i)),
                   pl.BlockSpec((TOP_K, tm), lambda i: (0, i))],
        out_shape=[jax.ShapeDtypeStruct((n, d), F32), jax.ShapeDtypeStruct((TOP_K, n), jnp.int32),
                   jax.ShapeDtypeStruct((TOP_K, n), F32)],
        compiler_params=_cparams("parallel"),
        name="moe_router",
    )(x, g, router_t)


def _gather_rows_kernel(idx_ref, src_ref, o_ref, sem):
    rows = o_ref.shape[0]

    def row_copy(r):
        return pltpu.make_async_copy(src_ref.at[pl.ds(idx_ref[0, r], 1)], o_ref.at[pl.ds(r, 1)], sem)

    def start(r, c):
        row_copy(r).start()
        return c

    def wait(r, c):
        row_copy(r).wait()
        return c

    lax.fori_loop(0, rows, start, 0, unroll=8)
    lax.fori_loop(0, rows, wait, 0, unroll=8)


def _gather_rows(src, idx, block):
    m = idx.shape[0]
    d = src.shape[1]
    return pl.pallas_call(
        _gather_rows_kernel,
        grid=(m // block,),
        in_specs=[pl.BlockSpec((None, 1, block), lambda i: (i, 0, 0), memory_space=pltpu.SMEM),
                  pl.BlockSpec(memory_space=pl.ANY)],
        out_specs=pl.BlockSpec((block, d), lambda i: (i, 0)),
        out_shape=jax.ShapeDtypeStruct((m, d), src.dtype),
        scratch_shapes=[pltpu.SemaphoreType.DMA(())],
        compiler_params=_cparams("arbitrary"),
        name="gather_rows",
    )(idx.reshape(m // block, 1, block), src)


def _experts_kernel(be_ref, act_ref, x_ref, wg_ref, wu_ref, wd_ref, o_ref, acc_scr):
    i = pl.program_id(0)
    j = pl.program_id(1)

    @pl.when(act_ref[i] > 0)
    def _():
        @pl.when(j == 0)
        def _():
            acc_scr[...] = jnp.zeros_like(acc_scr)

        xb = x_ref[...].astype(BF16)
        gate = jnp.dot(xb, wg_ref[...], preferred_element_type=F32)
        up = jnp.dot(xb, wu_ref[...], preferred_element_type=F32)
        acc_scr[...] += jnp.dot((_silu(gate) * up).astype(BF16), wd_ref[...], preferred_element_type=F32)

    @pl.when(j == pl.num_programs(1) - 1)
    def _():
        o_ref[...] = jnp.where(act_ref[i] > 0, acc_scr[...], 0.0)


def _experts(block_expert, block_active, xg, wg, wu, wd):
    m, d = xg.shape
    rb = MOE_ROW_BLOCK
    f = wg.shape[2]
    tf = _largest_tile(f, 1792, LANES)
    return pl.pallas_call(
        _experts_kernel,
        grid_spec=pltpu.PrefetchScalarGridSpec(
            num_scalar_prefetch=2,
            grid=(m // rb, f // tf),
            in_specs=[
                pl.BlockSpec((rb, d), lambda i, j, be, act: (i, 0)),
                pl.BlockSpec((None, d, tf), lambda i, j, be, act: (be[i], 0, j)),
                pl.BlockSpec((None, d, tf), lambda i, j, be, act: (be[i], 0, j)),
                pl.BlockSpec((None, tf, d), lambda i, j, be, act: (be[i], j, 0)),
            ],
            out_specs=pl.BlockSpec((rb, d), lambda i, j, be, act: (i, 0)),
            scratch_shapes=[pltpu.VMEM((rb, d), F32)],
        ),
        out_shape=jax.ShapeDtypeStruct((m, d), F32),
        compiler_params=_cparams("arbitrary", "arbitrary"),
        name="moe_experts",
    )(block_expert, block_active, xg, wg, wu, wd)


def _combine_kernel(x_ref, y0_ref, y1_ref, g0_ref, g1_ref, o_ref):
    o_ref[...] = x_ref[...] + (g0_ref[...] * y0_ref[...] + g1_ref[...] * y1_ref[...])


def _combine(x, y2, gates):
    n, d = x.shape
    tm = MOE_TOKEN_TILE
    nt = n // tm
    g0 = gates[0].reshape(n, 1)
    g1 = gates[1].reshape(n, 1)
    return pl.pallas_call(
        _combine_kernel,
        grid=(nt,),
        in_specs=[pl.BlockSpec((tm, d), lambda i: (i, 0)), pl.BlockSpec((tm, d), lambda i: (i, 0)),
                  pl.BlockSpec((tm, d), lambda i: (i + nt, 0)), pl.BlockSpec((tm, 1), lambda i: (i, 0)),
                  pl.BlockSpec((tm, 1), lambda i: (i, 0))],
        out_specs=pl.BlockSpec((tm, d), lambda i: (i, 0)),
        out_shape=jax.ShapeDtypeStruct((n, d), F32),
        compiler_params=_cparams("parallel"),
        name="moe_combine",
    )(x, y2, y2, g0, g1)


def _moe(x, norm, router_t, wg, wu, wd):
    n = x.shape[0]
    rb = MOE_ROW_BLOCK
    h, idx, gates = _router(x, norm.reshape(1, -1), router_t)
    expert = idx.reshape(-1)
    n_asg = expert.shape[0]
    onehot = (expert[:, None] == jnp.arange(N_EXPERTS, dtype=jnp.int32)[None, :]).astype(jnp.int32)
    pos = jnp.take_along_axis(jnp.cumsum(onehot, axis=0), expert[:, None], axis=1)[:, 0] - 1
    counts = jnp.sum(onehot, axis=0)
    padded = (counts + rb - 1) // rb * rb
    p_end = jnp.cumsum(padded)
    slot = (p_end - padded)[expert] + pos
    n_blocks = -(-n_asg // rb) + N_EXPERTS
    token = jnp.tile(jnp.arange(n, dtype=jnp.int32), TOP_K)
    tok_of_slot = jnp.zeros((n_blocks * rb,), jnp.int32).at[slot].set(token)
    starts = jnp.arange(n_blocks, dtype=jnp.int32) * rb
    block_expert = jnp.minimum(jnp.searchsorted(p_end, starts, side="right"), N_EXPERTS - 1).astype(jnp.int32)
    block_active = (starts < p_end[-1]).astype(jnp.int32)
    xg = _gather_rows(h, tok_of_slot, rb)
    yg = _experts(block_expert, block_active, xg, wg, wu, wd)
    y2 = _gather_rows(yg, slot.astype(jnp.int32), MOE_TOKEN_TILE)
    return _combine(x, y2, gates)


def _col(v):
    return v.reshape(-1, 1).astype(F32)


def _gm_weights(w_in, q_norm, k_norm):
    wt = w_in.T
    n_ga0 = 2 * GLA_HEADS * GLA_DK + GLA_HEADS * GLA_DV
    wt = jnp.concatenate([wt[:n_ga0], wt[n_ga0 + GLA_GATE_RANK:], wt[n_ga0:n_ga0 + GLA_GATE_RANK],
                          jnp.zeros((LANES - GLA_GATE_RANK, wt.shape[1]), wt.dtype)], axis=0).astype(BF16)
    ones = lambda n: jnp.ones((n,), F32)
    nm = MOBA_HEADS * HEAD_DIM
    gain = jnp.concatenate([ones(256) * GLA_DK ** -0.5, ones(256), ones(512), ones(512),
                            jnp.tile(q_norm, MOBA_HEADS) * HEAD_DIM ** -0.5, jnp.tile(k_norm, MOBA_HEADS),
                            ones(nm), ones(LANES)])
    nflag = jnp.concatenate([jnp.zeros((1536,), F32), ones(2 * nm), jnp.zeros((nm + LANES,), F32)])
    return wt, _col(nflag), _col(gain)


GM_ROWS = dict(qg=0, kg=256, vg=512, rg=1024, qm=1536, km=2048, vm=2560, ga=3072)


def _gm_prompt(xp, bsz, seq, norm_mix, wts, gate_w, gate_b, out_norm, w_out, tm):
    wt, nflag, gain = wts
    r = GM_ROWS
    yt = _proj_t(xp, 0, bsz, seq, norm_mix.reshape(1, -1), wt, nflag, gain, tm)
    gw_t = gate_w.T.reshape(GLA_HEADS, GLA_DK, GLA_GATE_RANK)
    gb = gate_b.reshape(GLA_HEADS, GLA_DK, 1)
    on = out_norm.reshape(GLA_HEADS, GLA_DV, 1)
    og, s_t = _gla_prompt(yt, (r["qg"], r["kg"], r["vg"], r["rg"], r["ga"]), gw_t, gb, on)
    om = _moba_prompt(yt, _alibi_slopes(MOBA_HEADS), r["qm"], r["km"], r["vm"])
    xp = _outproj_t(xp, 0, [og, om], w_out.astype(BF16), tm)
    return xp, s_t, yt


NS_ROWS = dict(q=0, kc=1024, vc=1280, ks=1536, vs=1792, kw=2048, vw=2304, g=2560)


def _ns_weights(w_in, q_norm, ks_norm, kw_norm):
    wt = w_in.T
    n_g = NSA_HEADS * 3
    wt = jnp.concatenate([wt, jnp.zeros((LANES - n_g, wt.shape[1]), wt.dtype)], axis=0).astype(BF16)
    ones = lambda n: jnp.ones((n,), F32)
    zeros = lambda n: jnp.zeros((n,), F32)
    kvw = NSA_KV_HEADS * HEAD_DIM
    gain = jnp.concatenate([jnp.tile(q_norm, NSA_HEADS) * HEAD_DIM ** -0.5, ones(2 * kvw),
                            jnp.tile(ks_norm, NSA_KV_HEADS), ones(kvw),
                            jnp.tile(kw_norm, NSA_KV_HEADS), ones(kvw), ones(LANES)])
    nflag = jnp.concatenate([ones(NSA_HEADS * HEAD_DIM), zeros(2 * kvw), ones(kvw), zeros(kvw), ones(kvw),
                             zeros(kvw), zeros(LANES)])
    return wt, _col(nflag), _col(gain)


def _ns_prompt(xp, bsz, seq, norm_mix, wts, gate_b, kc_norm, cmpk, cmpv, w_out, tm):
    wt, nflag, gain = wts
    r = NS_ROWS
    yt = _proj_t(xp, 0, bsz, seq, norm_mix.reshape(1, -1), wt, nflag, gain, tm)
    kcmp_t = _cmp_stage2(_cmp_stage1_dense(yt, r["kc"], _cmp_w1cat(cmpk[1])), *cmpk, kc_norm)
    vcmp_t = _cmp_stage2(_cmp_stage1_dense(yt, r["vc"], _cmp_w1cat(cmpv[1])), *cmpv, None)
    gb = jnp.pad(gate_b, (0, LANES - gate_b.shape[0])).reshape(LANES, 1)
    o_t = _nsa_prompt(yt, kcmp_t, vcmp_t, _alibi_slopes(NSA_HEADS),
                      (r["q"], r["ks"], r["vs"], r["kw"], r["vw"], r["g"]), gb)
    xp = _outproj_t(xp, 0, [o_t], w_out.astype(BF16), tm)
    return xp, yt


def _sample_rows(xs, norm, wts):
    n = xs.shape[0]
    wt, nflag, gain = wts
    xs_pad = jnp.pad(xs, ((0, LANES - n), (0, 0)))
    yt = _proj_t(xs_pad, 0, 1, LANES, norm.reshape(1, -1), wt, nflag, gain, LANES)
    return yt[0, :, :n].T


def _ns_sample(xs, norm_mix, wts, gate_b, kc_norm, cmpk, cmpv, w_out, pools, win_k, win_v, page_table):
    n = xs.shape[0]
    r = NS_ROWS
    past = page_table.shape[1] * LANES
    ys = _sample_rows(xs, norm_mix, wts)
    kvw = NSA_KV_HEADS * HEAD_DIM
    new = {nm: ys[:, r[nm]:r[nm] + kvw].reshape(n, NSA_KV_HEADS, HEAD_DIM) for nm in ("kc", "vc", "ks", "vs", "kw", "vw")}
    colv = lambda a: a.reshape(n, NSA_KV_HEADS, HEAD_DIM, 1)
    q = ys[:, :NSA_HEADS * HEAD_DIM].reshape(n, NSA_KV_HEADS, NSA_GROUP, HEAD_DIM, 1)
    gt = ys[:, r["g"]:r["g"] + 3 * NSA_HEADS].reshape(n, NSA_KV_HEADS, 3 * NSA_GROUP, 1)
    gb = gate_b.reshape(NSA_KV_HEADS, 3 * NSA_GROUP, 1)
    view = lambda a: jnp.transpose(a, (0, 2, 3, 1))
    ck_t, cv_t, sk_t, sv_t = (view(p) for p in pools)
    kcmp_t = _cmp_stage2(_cmp_stage1_paged(page_table, ck_t, _cmp_w1cat(cmpk[1])), *cmpk, kc_norm)
    vcmp_t = _cmp_stage2(_cmp_stage1_paged(page_table, cv_t, _cmp_w1cat(cmpv[1])), *cmpv, None)
    slopes = _alibi_slopes(NSA_HEADS)
    o_c, sel = _nsa_sample_cmp(q, kcmp_t, vcmp_t, slopes, past, past + 1)
    sel = sel[:, :, 0, :NSA_SLC_TOPN].reshape(n, NSA_KV_HEADS * NSA_SLC_TOPN)
    o = _nsa_sample_attn(page_table, sel, slopes, q, o_c, gt, gb,
                         [colv(new[nm]) for nm in ("ks", "vs", "kw", "vw")], view(win_k), view(win_v), sk_t, sv_t)
    xs = _outproj_r(xs, o.reshape(n, NSA_HEADS * HEAD_DIM), w_out.astype(BF16))
    return xs, new


def _gm_sample(xs, norm_mix, wts, gate_w, gate_b, out_norm, w_out, state, pool_k, pool_v, page_table):
    n = xs.shape[0]
    r = GM_ROWS
    ys = _sample_rows(xs, norm_mix, wts)
    hk, hv, nm = GLA_HEADS * GLA_DK, GLA_HEADS * GLA_DV, MOBA_HEADS * HEAD_DIM
    col = lambda r0, heads, dim: ys[:, r0:r0 + heads * dim].reshape(n, heads, dim, 1)
    row = lambda r0, heads, dim: ys[:, r0:r0 + heads * dim].reshape(n, heads, 1, dim)
    gw_t = gate_w.T.reshape(GLA_HEADS, GLA_DK, GLA_GATE_RANK)
    og, s_new = _gla_sample(
        col(r["qg"], GLA_HEADS, GLA_DK), col(r["kg"], GLA_HEADS, GLA_DK), row(r["vg"], GLA_HEADS, GLA_DV),
        row(r["rg"], GLA_HEADS, GLA_DV), ys[:, r["ga"]:r["ga"] + GLA_GATE_RANK].reshape(n, 1, 1, GLA_GATE_RANK),
        gw_t, gate_b.reshape(GLA_HEADS, GLA_DK, 1), out_norm.reshape(GLA_HEADS, 1, GLA_DV), state)
    qm, km, vm = (col(r[nme], MOBA_HEADS, HEAD_DIM) for nme in ("qm", "km", "vm"))
    kpool_t = jnp.transpose(pool_k, (0, 2, 3, 1))
    vpool_t = jnp.transpose(pool_v, (0, 2, 3, 1))
    _, top = _moba_gates(page_table, qm, kpool_t)
    top = top[:, :, :MOBA_TOPK].reshape(n, MOBA_HEADS * MOBA_TOPK)
    om = _moba_sample(page_table, top, _alibi_slopes(MOBA_HEADS), qm, km, vm, kpool_t, vpool_t)
    a = jnp.concatenate([og.reshape(n, hv), om.reshape(n, nm)], axis=1)
    xs = _outproj_r(xs, a, w_out.astype(BF16))
    return xs, s_new, km.reshape(n, MOBA_HEADS, HEAD_DIM), vm.reshape(n, MOBA_HEADS, HEAD_DIM)


def _rows_to_cache(yt, row0, heads):
    bsz, _, seq = yt.shape
    a = yt[:, row0:row0 + heads * HEAD_DIM].reshape(bsz, heads, HEAD_DIM, seq)
    return jnp.transpose(a, (0, 3, 1, 2))


def kernel(x_prompt, x_sample, state_gla, cache_moba_k, cache_moba_v, cache_nsa_cmp_k, cache_nsa_cmp_v, cache_nsa_slc_k, cache_nsa_slc_v, state_nsa_win_k, state_nsa_win_v, page_table, gm_norm_mix, gm_w_in, gm_gla_gate_w, gm_gla_gate_b, gm_gla_out_norm, gm_moba_q_norm, gm_moba_k_norm, gm_w_out, gm_norm_ffn, gm_ffn_gate, gm_ffn_up, gm_ffn_down, ns_norm_mix, ns_w_in, ns_gate_b, ns_q_norm, ns_kcmp_norm, ns_kslc_norm, ns_kwin_norm, ns_cmpk_pe, ns_cmpk_w1, ns_cmpk_b1, ns_cmpk_w2, ns_cmpk_b2, ns_cmpv_pe, ns_cmpv_w1, ns_cmpv_b1, ns_cmpv_w2, ns_cmpv_b2, ns_w_out, ns_norm_ffn, ns_router, ns_exp_gate, ns_exp_up, ns_exp_down):
    bsz, seq, d = x_prompt.shape
    n_s = x_sample.shape[0]
    assert x_sample.shape[1] == 1 and gm_w_in.shape[0] == 1 and ns_w_in.shape[0] == 1
    tm = _largest_tile(seq, 512, LANES)
    xp = x_prompt.reshape(bsz * seq, d)
    xs = x_sample.reshape(n_s, d)

    i = 0
    wts = _gm_weights(gm_w_in[i], gm_moba_q_norm[i], gm_moba_k_norm[i])
    xp, gla_p, yt0 = _gm_prompt(xp, bsz, seq, gm_norm_mix[i], wts, gm_gla_gate_w[i], gm_gla_gate_b[i],
                                gm_gla_out_norm[i], gm_w_out[i], tm)
    xs, gla_s, mk_s, mv_s = _gm_sample(xs, gm_norm_mix[i], wts, gm_gla_gate_w[i], gm_gla_gate_b[i],
                                       gm_gla_out_norm[i], gm_w_out[i], state_gla[i], cache_moba_k[i],
                                       cache_moba_v[i], page_table)
    ffn = (gm_norm_ffn[i].reshape(1, -1), gm_ffn_gate[i].astype(BF16), gm_ffn_up[i].astype(BF16),
           gm_ffn_down[i].astype(BF16))
    xp = _ffn(xp, *ffn, tm)
    xs = _ffn(xs, *ffn, n_s)

    wts = _ns_weights(ns_w_in[i], ns_q_norm[i], ns_kslc_norm[i], ns_kwin_norm[i])
    cmpk = (ns_cmpk_pe[i], ns_cmpk_w1[i], ns_cmpk_b1[i], ns_cmpk_w2[i], ns_cmpk_b2[i])
    cmpv = (ns_cmpv_pe[i], ns_cmpv_w1[i], ns_cmpv_b1[i], ns_cmpv_w2[i], ns_cmpv_b2[i])
    xp, yt1 = _ns_prompt(xp, bsz, seq, ns_norm_mix[i], wts, ns_gate_b[i], ns_kcmp_norm[i], cmpk, cmpv,
                         ns_w_out[i], tm)
    pools = (cache_nsa_cmp_k[i], cache_nsa_cmp_v[i], cache_nsa_slc_k[i], cache_nsa_slc_v[i])
    xs, new = _ns_sample(xs, ns_norm_mix[i], wts, ns_gate_b[i], ns_kcmp_norm[i], cmpk, cmpv, ns_w_out[i], pools,
                         state_nsa_win_k[i], state_nsa_win_v[i], page_table)
    n_tok = bsz * seq + n_s
    n_pad = -(-n_tok // MOE_TOKEN_TILE) * MOE_TOKEN_TILE
    x_all = jnp.concatenate([xp, xs, jnp.zeros((n_pad - n_tok, d), F32)], axis=0)
    x_all = _moe(x_all, ns_norm_ffn[i], ns_router[i].T, ns_exp_gate[i].astype(BF16), ns_exp_up[i].astype(BF16),
                 ns_exp_down[i].astype(BF16))
    y_prompt = x_all[:bsz * seq].reshape(bsz, seq, d)
    y_sample = x_all[bsz * seq:n_tok].reshape(n_s, 1, d)

    r0, r1 = GM_ROWS, NS_ROWS
    n_win = min(NSA_WINDOW, seq)
    ns_p = {nm: _rows_to_cache(yt1, r1[nm], NSA_KV_HEADS) for nm in ("kc", "vc", "ks", "vs", "kw", "vw")}
    win_s = lambda state, row: jnp.concatenate([state, row[:, None]], axis=1)[:, -min(NSA_WINDOW, state.shape[1] + 1):]
    lead = lambda a: a[None]
    tok = lambda a: a[None, :, None]
    return (y_prompt, y_sample, lead(gla_p), lead(gla_s),
            lead(_rows_to_cache(yt0, r0["km"], MOBA_HEADS)), lead(_rows_to_cache(yt0, r0["vm"], MOBA_HEADS)),
            tok(mk_s), tok(mv_s),
            lead(ns_p["kc"]), lead(ns_p["vc"]), lead(ns_p["ks"]), lead(ns_p["vs"]),
            lead(ns_p["kw"][:, -n_win:]), lead(ns_p["vw"][:, -n_win:]),
            tok(new["kc"]), tok(new["vc"]), tok(new["ks"]), tok(new["vs"]),
            lead(win_s(state_nsa_win_k[i], new["kw"])), lead(win_s(state_nsa_win_v[i], new["vw"])))
```

```python
import functools

import numpy as np
import jax
import jax.numpy as jnp
from jax import lax
from jax.experimental import pallas as pl
from jax.experimental.pallas import tpu as pltpu

F32 = jnp.float32
BF16 = jnp.bfloat16

HEAD_DIM = 64
GLA_HEADS, GLA_DK, GLA_DV, GLA_GATE_RANK, GLA_TAU, GLA_CHUNK = 4, 64, 128, 16, 16.0, 16
MOBA_HEADS, MOBA_BLOCK, MOBA_TOPK = 8, 256, 3
NSA_HEADS, NSA_KV_HEADS = 16, 4
NSA_GROUP = NSA_HEADS // NSA_KV_HEADS
NSA_CMP_LEN, NSA_CMP_STRIDE, NSA_CMP_HIDDEN = 32, 16, 64
NSA_SLC_BLOCK, NSA_SLC_TOPN, NSA_INIT_BLOCKS, NSA_LOCAL_BLOCKS, NSA_WINDOW = 64, 16, 1, 2, 512
N_EXPERTS, TOP_K = 8, 2
QUERY_BLOCK = 16
RMS_EPS = 1e-6
NEG_INF = -1e30
FORCE_SCORE = 1e9

LANES = 128
SUBLANES = 8
VMEM_LIMIT_BYTES = 56 * 1024 * 1024
WIDE_TOKEN_TILE = 1024

_NT = (((1,), (1,)), ((), ()))
_TN = (((0,), (0,)), ((), ()))
_HI = lax.Precision.HIGHEST


def _cparams(*sem):
    return pltpu.CompilerParams(dimension_semantics=sem, vmem_limit_bytes=VMEM_LIMIT_BYTES)


def _largest_tile(n, cap, mult):
    t = (cap // mult) * mult
    while t > mult and n % t:
        t -= mult
    assert n % t == 0, (n, cap, mult)
    return t


def _alibi_slopes(n):
    return 2.0 ** (-8.0 * jnp.arange(1, n + 1, dtype=F32) / n)


def _silu(x):
    return x * (1.0 / (1.0 + jnp.exp(-x)))


def _loop_tiles(lo, hi, fn, carry, groups=(4, 2, 1)):
    for group in groups:
        n = (hi - lo) // group

        def trip(i, c, lo=lo, group=group):
            for u in range(group):
                c = fn(lo + group * i + u, c)
            return c

        carry = lax.fori_loop(0, n, trip, carry)
        lo = lo + n * group
    return carry


def _r16(x):
    return x.astype(BF16).astype(F32)


def _proj_t_kernel(x_ref, g_ref, wt_ref, nflag_ref, gain_ref, o_ref, h_scr):
    @pl.when(pl.program_id(1) == 0)
    def _():
        x = x_ref[...]
        ms = jnp.mean(x * x, axis=-1, keepdims=True)
        h_scr[...] = (x * lax.rsqrt(ms + RMS_EPS) * g_ref[...]).astype(BF16)

    y = lax.dot_general(wt_ref[...], h_scr[...], _NT, preferred_element_type=F32)
    tc, tm = y.shape
    y3 = y.reshape(tc // HEAD_DIM, HEAD_DIM, tm)
    ms = jnp.mean(y3 * y3, axis=1, keepdims=True)
    nf = nflag_ref[...].reshape(tc // HEAD_DIM, HEAD_DIM, 1)
    gn = gain_ref[...].reshape(tc // HEAD_DIM, HEAD_DIM, 1)
    scale = jnp.where(nf > 0.0, lax.rsqrt(ms + RMS_EPS), 1.0) * gn
    o_ref[...] = (y3 * scale).reshape(tc, tm)


def _proj_t(x, row0, bsz, seq, g, wt, nflag, gain, tm):
    d = x.shape[1]
    c = wt.shape[0]
    tc = _largest_tile(c, 640, LANES)
    nt = seq // tm
    return pl.pallas_call(
        _proj_t_kernel,
        grid=(bsz * nt, c // tc),
        in_specs=[
            pl.BlockSpec((tm, d), lambda i, j: (row0 // tm + i, 0)),
            pl.BlockSpec((1, d), lambda i, j: (0, 0)),
            pl.BlockSpec((tc, d), lambda i, j: (j, 0)),
            pl.BlockSpec((tc, 1), lambda i, j: (j, 0)),
            pl.BlockSpec((tc, 1), lambda i, j: (j, 0)),
        ],
        out_specs=pl.BlockSpec((None, tc, tm), lambda i, j: (i // nt, j, i % nt)),
        out_shape=jax.ShapeDtypeStruct((bsz, c, seq), F32),
        scratch_shapes=[pltpu.VMEM((tm, d), BF16)],
        compiler_params=_cparams("parallel", "arbitrary"),
        name="proj_t",
    )(x, g, wt, nflag, gain)


def _outproj_t_kernel(*refs, n_in):
    x_ref, a_refs, w_ref, o_ref = refs[0], refs[1:1 + n_in], refs[1 + n_in], refs[2 + n_in]
    a = jnp.concatenate([r[...].astype(BF16) for r in a_refs], axis=0)
    o_ref[...] = x_ref[...] + lax.dot_general(a, w_ref[...], _TN, preferred_element_type=F32)


def _outproj_t(x, row0, a_list, w, tm):
    d = x.shape[1]
    bsz, _, seq = a_list[0].shape
    nt = seq // tm
    a_specs = [pl.BlockSpec((None, a.shape[1], tm), lambda i: (i // nt, 0, i % nt)) for a in a_list]
    return pl.pallas_call(
        functools.partial(_outproj_t_kernel, n_in=len(a_list)),
        grid=(bsz * nt,),
        in_specs=[pl.BlockSpec((tm, d), lambda i: (row0 // tm + i, 0))] + a_specs
        + [pl.BlockSpec(w.shape, lambda i: (0, 0))],
        out_specs=pl.BlockSpec((tm, d), lambda i: (i, 0)),
        out_shape=jax.ShapeDtypeStruct((bsz * seq, d), F32),
        compiler_params=_cparams("parallel"),
        name="outproj_t",
    )(x, *a_list, w)


def _outproj_r_kernel(x_ref, a_ref, w_ref, o_ref):
    o_ref[...] = x_ref[...] + jnp.dot(a_ref[...].astype(BF16), w_ref[...], preferred_element_type=F32)


def _outproj_r(x, a, w):
    return pl.pallas_call(
        _outproj_r_kernel,
        out_shape=jax.ShapeDtypeStruct(x.shape, F32),
        compiler_params=pltpu.CompilerParams(vmem_limit_bytes=VMEM_LIMIT_BYTES),
        name="outproj_r",
    )(x, a, w)


def _ffn_kernel(x_ref, g_ref, wg_ref, wu_ref, wd_ref, o_ref, h_scr, acc_scr):
    j = pl.program_id(1)

    @pl.when(j == 0)
    def _():
        x = x_ref[...]
        ms = jnp.mean(x * x, axis=-1, keepdims=True)
        h_scr[...] = (x * lax.rsqrt(ms + RMS_EPS) * g_ref[...]).astype(BF16)
        acc_scr[...] = jnp.zeros_like(acc_scr)

    h = h_scr[...]
    gate = jnp.dot(h, wg_ref[...], preferred_element_type=F32)
    up = jnp.dot(h, wu_ref[...], preferred_element_type=F32)
    act = (_silu(gate) * up).astype(BF16)
    acc_scr[...] += jnp.dot(act, wd_ref[...], preferred_element_type=F32)

    @pl.when(j == pl.num_programs(1) - 1)
    def _():
        o_ref[...] = x_ref[...] + acc_scr[...]


def _ffn(x, g, wg, wu, wd, tm):
    n, d = x.shape
    f = wg.shape[1]
    tf = _largest_tile(f, 1536, LANES)
    return pl.pallas_call(
        _ffn_kernel,
        grid=(n // tm, f // tf),
        in_specs=[
            pl.BlockSpec((tm, d), lambda i, j: (i, 0)),
            pl.BlockSpec((1, d), lambda i, j: (0, 0)),
            pl.BlockSpec((d, tf), lambda i, j: (0, j)),
            pl.BlockSpec((d, tf), lambda i, j: (0, j)),
            pl.BlockSpec((tf, d), lambda i, j: (j, 0)),
        ],
        out_specs=pl.BlockSpec((tm, d), lambda i, j: (i, 0)),
        out_shape=jax.ShapeDtypeStruct((n, d), F32),
        scratch_shapes=[pltpu.VMEM((tm, d), BF16), pltpu.VMEM((tm, d), F32)],
        compiler_params=_cparams("parallel", "arbitrary"),
        name="ffn",
    )(x, g, wg, wu, wd)


def _moba_p_kernel(slopes_ref, q_ref, k_ref, v_ref, o_ref, sel_scr, s_scr, km_scr, *, n_blk):
    h = pl.program_id(1)
    qi = pl.program_id(2)
    blk = MOBA_BLOCK
    slope = slopes_ref[h]
    q = q_ref[...]
    qb = q.astype(BF16)

    @pl.when(qi == 0)
    def _():
        for j in range(n_blk):
            km_scr[:, j:j + 1] = _r16(jnp.mean(k_ref[:, j * blk:(j + 1) * blk], axis=1, keepdims=True))

    gates = []
    for j in range(n_blk):
        gates.append(jnp.sum(qb.astype(F32) * km_scr[:, j:j + 1], axis=0, keepdims=True))
    for j in range(n_blk):
        cnt = jnp.zeros_like(gates[0])
        for j2 in range(n_blk):
            if j2 == j:
                continue
            beats = (gates[j2] >= gates[j]) if j2 < j else (gates[j2] > gates[j])
            cnt = cnt + jnp.where(beats, 1.0, 0.0) * (j2 < qi).astype(F32)
        sel_scr[j:j + 1, :] = jnp.where(cnt < float(MOBA_TOPK), 0.0, -NEG_INF)

    sub = lax.broadcasted_iota(jnp.int32, (blk, blk), 0)
    lane = lax.broadcasted_iota(jnp.int32, (blk, blk), 1)
    rel = (lane - sub).astype(F32)
    alibi = slope * rel

    def scores(j, penalty, row_const, m):
        start = pl.multiple_of(j * blk, blk)
        kj = k_ref[:, pl.ds(start, blk)].astype(BF16)
        s = lax.dot_general(kj, qb, _TN, preferred_element_type=F32) - penalty - row_const
        s_scr[pl.ds(start, blk), :] = s
        return jnp.maximum(m, jnp.max(s, axis=0, keepdims=True))

    def past(j, m):
        row_const = slope * ((qi - j) * blk).astype(F32) + sel_scr[pl.ds(j, 1), :]
        return scores(j, alibi, row_const, m)

    m = scores(qi, jnp.where(rel >= 0.0, alibi, -NEG_INF), 0.0, jnp.full((1, blk), NEG_INF, F32))
    m = _loop_tiles(0, qi, past, m)

    def accumulate(j, carry):
        l, acc = carry
        start = pl.multiple_of(j * blk, blk)
        p = jnp.exp(s_scr[pl.ds(start, blk), :] - m)
        vj = v_ref[:, pl.ds(start, blk)].astype(BF16)
        return (l + jnp.sum(p, axis=0, keepdims=True),
                acc + jnp.dot(vj, p.astype(BF16), preferred_element_type=F32))

    l, acc = _loop_tiles(0, qi + 1, accumulate, (jnp.zeros((1, blk), F32), jnp.zeros((HEAD_DIM, blk), F32)))
    o_ref[...] = acc / l


def _moba_prompt(yt, slopes, q_row0, k_row0, v_row0):
    bsz, _, seq = yt.shape
    assert seq % MOBA_BLOCK == 0
    n_blk = seq // MOBA_BLOCK
    hd = HEAD_DIM

    def kvspec(row0):
        return pl.BlockSpec((None, hd, seq), lambda b, h, i: (b, row0 // hd + h, 0))

    return pl.pallas_call(
        functools.partial(_moba_p_kernel, n_blk=n_blk),
        grid=(bsz, MOBA_HEADS, n_blk),
        in_specs=[
            pl.BlockSpec(memory_space=pltpu.SMEM),
            pl.BlockSpec((None, hd, MOBA_BLOCK), lambda b, h, i: (b, q_row0 // hd + h, i)),
            kvspec(k_row0),
            kvspec(v_row0),
        ],
        out_specs=pl.BlockSpec((None, hd, MOBA_BLOCK), lambda b, h, i: (b, h, i)),
        out_shape=jax.ShapeDtypeStruct((bsz, MOBA_HEADS * hd, seq), F32),
        scratch_shapes=[pltpu.VMEM((max(n_blk, SUBLANES), MOBA_BLOCK), F32), pltpu.VMEM((seq, MOBA_BLOCK), F32),
                        pltpu.VMEM((HEAD_DIM, LANES), F32)],
        compiler_params=_cparams("parallel", "parallel", "arbitrary"),
        name="moba_prompt",
    )(slopes, yt, yt, yt)


def _log_sigmoid(x):
    return jnp.minimum(x, 0.0) - jnp.log1p(jnp.exp(-jnp.abs(x)))


def _gla_p_kernel(q_ref, k_ref, v_ref, ga_ref, rg_ref, gw_ref, gb_ref, on_ref, o_ref, s_ref, st_scr):
    ti = pl.program_id(1)
    ch = GLA_CHUNK
    n_ch = LANES // ch

    @pl.when(ti == 0)
    def _():
        st_scr[...] = jnp.zeros_like(st_scr)

    ga = ga_ref[0:GLA_GATE_RANK, :].astype(BF16)
    lane_k = lax.broadcasted_iota(jnp.int32, (GLA_DK, LANES), 1)
    pos = lane_k % ch
    cid_v = lax.broadcasted_iota(jnp.int32, (GLA_DV, LANES), 1) // ch
    cid_k = lane_k // ch

    def head(hh):
        q = q_ref[hh * GLA_DK:(hh + 1) * GLA_DK, :]
        k = k_ref[hh * GLA_DK:(hh + 1) * GLA_DK, :]
        v = v_ref[hh * GLA_DV:(hh + 1) * GLA_DV, :]
        x = jnp.dot(gw_ref[hh].astype(BF16), ga, preferred_element_type=F32) + gb_ref[hh]
        g = _log_sigmoid(x) / GLA_TAU

        b = g
        c = g
        for sh in (1, 2, 4, 8):
            b = b + jnp.where(pos >= sh, pltpu.roll(b, sh, 1), 0.0)
            c = c + jnp.where(pos < ch - sh, pltpu.roll(c, LANES - sh, 1), 0.0)
        tot = b + c - g

        qd = q * jnp.exp(b)
        kd = k * jnp.exp(tot - b)

        o = jnp.zeros((GLA_DV, LANES), F32)
        for dl in range(ch):
            if dl == 0:
                kk, bb, vv = k, b, v
            else:
                kk, bb, vv = pltpu.roll(k, dl, 1), pltpu.roll(b, dl, 1), pltpu.roll(v, dl, 1)
            a = jnp.where(pos >= dl, q * kk * jnp.exp(b - bb), 0.0)
            o = o + jnp.sum(a, axis=0, keepdims=True) * vv

        vm = jnp.concatenate([jnp.where(cid_v == cc, v, 0.0) for cc in range(n_ch)], axis=0).astype(BF16)
        ut = lax.dot_general(vm, kd.astype(BF16), _NT, preferred_element_type=F32)
        dt = jnp.exp(tot).T
        st = st_scr[hh]
        pieces = []
        for cc in range(n_ch):
            pieces.append(st.astype(BF16))
            st = st * dt[cc * ch:cc * ch + 1, :] + ut[cc * GLA_DV:(cc + 1) * GLA_DV, :]
        st_scr[hh] = st
        sstack = jnp.concatenate(pieces, axis=1)
        qm = jnp.concatenate([jnp.where(cid_k == cc, qd, 0.0) for cc in range(n_ch)], axis=0).astype(BF16)
        o = o + jnp.dot(sstack, qm, preferred_element_type=F32)

        ms = jnp.mean(o * o, axis=0, keepdims=True)
        rg = rg_ref[hh * GLA_DV:(hh + 1) * GLA_DV, :]
        o_ref[hh * GLA_DV:(hh + 1) * GLA_DV, :] = (o * lax.rsqrt(ms + RMS_EPS) * on_ref[hh]) * _silu(rg)

        @pl.when(ti == pl.num_programs(1) - 1)
        def _():
            s_ref[hh] = st.T

    for hh in range(GLA_HEADS):
        head(hh)


def _gla_prompt(yt, rows, gw_t, gb, on):
    bsz, _, seq = yt.shape
    nt = seq // LANES
    q0, k0, v0, rg0, ga0 = rows
    nk, nv = GLA_HEADS * GLA_DK, GLA_HEADS * GLA_DV
    assert q0 % nk == 0 and k0 % nk == 0 and v0 % nv == 0 and rg0 % nv == 0 and ga0 % LANES == 0

    def at(row0, size):
        return pl.BlockSpec((None, size, LANES), lambda b, t: (b, row0 // size, t))

    full = lambda a: pl.BlockSpec(a.shape, lambda b, t: (0, 0, 0))
    return pl.pallas_call(
        _gla_p_kernel,
        grid=(bsz, nt),
        in_specs=[at(q0, nk), at(k0, nk), at(v0, nv), at(ga0, LANES), at(rg0, nv), full(gw_t), full(gb), full(on)],
        out_specs=[
            pl.BlockSpec((None, nv, LANES), lambda b, t: (b, 0, t)),
            pl.BlockSpec((None, GLA_HEADS, GLA_DK, GLA_DV), lambda b, t: (b, 0, 0, 0)),
        ],
        out_shape=[
            jax.ShapeDtypeStruct((bsz, nv, seq), F32),
            jax.ShapeDtypeStruct((bsz, GLA_HEADS, GLA_DK, GLA_DV), F32),
        ],
        scratch_shapes=[pltpu.VMEM((GLA_HEADS, GLA_DV, GLA_DK), F32)],
        compiler_params=_cparams("parallel", "arbitrary"),
        name="gla_prompt",
    )(yt, yt, yt, yt, yt, gw_t, gb, on)


def _gla_s_kernel(q_ref, k_ref, v_ref, rg_ref, ga_ref, gw_ref, gb_ref, on_ref, s0_ref, o_ref, s_ref):
    x = jnp.sum(gw_ref[...] * ga_ref[...], axis=-1, keepdims=True) + gb_ref[...]
    a = jnp.exp(_log_sigmoid(x) / GLA_TAU)
    s = a * s0_ref[...] + k_ref[...] * v_ref[...]
    s_ref[...] = s
    o = jnp.sum(q_ref[...] * s, axis=1, keepdims=True)
    ms = jnp.mean(o * o, axis=-1, keepdims=True)
    o_ref[...] = (o * lax.rsqrt(ms + RMS_EPS) * on_ref[...]) * _silu(rg_ref[...])


def _gla_sample(q, k, v, rg, ga, gw_t, gb, on, s0):
    bsz = q.shape[0]
    per_b = lambda a: pl.BlockSpec((None,) + a.shape[1:], lambda b: (b, 0, 0, 0))
    full = lambda a: pl.BlockSpec(a.shape, lambda b: (0,) * a.ndim)
    return pl.pallas_call(
        _gla_s_kernel,
        grid=(bsz,),
        in_specs=[per_b(q), per_b(k), per_b(v), per_b(rg), per_b(ga), full(gw_t), full(gb), full(on), per_b(s0)],
        out_specs=[per_b(v), per_b(s0)],
        out_shape=[jax.ShapeDtypeStruct(v.shape, F32), jax.ShapeDtypeStruct(s0.shape, F32)],
        compiler_params=_cparams("parallel"),
        name="gla_sample",
    )(q, k, v, rg, ga, gw_t, gb, on, s0)


MOBA_PAGES_PER_STEP = 16


def _moba_gate_kernel(pt_ref, q_ref, *refs, n_blk):
    pages, (gate_ref, top_ref) = refs[:MOBA_PAGES_PER_STEP], refs[MOBA_PAGES_PER_STEP:]
    s = pl.program_id(1)
    per_step = MOBA_PAGES_PER_STEP // 2
    lane = lax.broadcasted_iota(jnp.int32, (MOBA_HEADS, LANES), 1)

    @pl.when(s == 0)
    def _():
        gate_ref[...] = jnp.full_like(gate_ref, NEG_INF)

    q = _r16(q_ref[...])
    gate = gate_ref[...]
    for u in range(per_step):
        ksum = pages[2 * u][...] + pages[2 * u + 1][...]
        kmean = jnp.sum(ksum, axis=-1, keepdims=True) / float(MOBA_BLOCK)
        gv = jnp.sum(q * _r16(kmean), axis=1)
        gate = jnp.where(lane == s * per_step + u, gv, gate)
    gate_ref[...] = gate

    @pl.when(s == pl.num_programs(1) - 1)
    def _():
        g = gate
        top = jnp.zeros((MOBA_HEADS, LANES), jnp.int32)
        for r in range(MOBA_TOPK):
            m = jnp.max(g, axis=-1, keepdims=True)
            idx = jnp.min(jnp.where(g == m, lane, LANES), axis=-1, keepdims=True)
            top = jnp.where(lane == r, idx, top)
            g = jnp.where(lane == idx, -jnp.inf, g)
        top_ref[...] = top


def _moba_gates(page_table, q, kpool_t):
    bsz, n_pages = page_table.shape
    n_blk = n_pages // 2
    assert MOBA_TOPK <= n_blk <= LANES and n_pages % MOBA_PAGES_PER_STEP == 0
    pps = MOBA_PAGES_PER_STEP

    def page_spec(u):
        return pl.BlockSpec((None, MOBA_HEADS, HEAD_DIM, LANES), lambda b, s, pt: (pt[b, s * pps + u], 0, 0, 0))

    out_spec = pl.BlockSpec((None, MOBA_HEADS, LANES), lambda b, s, pt: (b, 0, 0))
    return pl.pallas_call(
        functools.partial(_moba_gate_kernel, n_blk=n_blk),
        grid_spec=pltpu.PrefetchScalarGridSpec(
            num_scalar_prefetch=1,
            grid=(bsz, n_pages // pps),
            in_specs=[pl.BlockSpec((None, MOBA_HEADS, HEAD_DIM, 1), lambda b, s, pt: (b, 0, 0, 0))]
            + [page_spec(u) for u in range(pps)],
            out_specs=[out_spec, out_spec],
        ),
        out_shape=[jax.ShapeDtypeStruct((bsz, MOBA_HEADS, LANES), F32),
                   jax.ShapeDtypeStruct((bsz, MOBA_HEADS, LANES), jnp.int32)],
        compiler_params=_cparams("parallel", "arbitrary"),
        name="moba_gates",
    )(page_table, q, *([kpool_t] * pps))


def _moba_s_kernel(pt_ref, top_ref, slopes_ref, q_ref, kn_ref, vn_ref, *refs, past):
    n_pg = 2 * MOBA_TOPK
    k_refs, v_refs, o_ref = refs[:n_pg], refs[n_pg:2 * n_pg], refs[2 * n_pg]
    b = pl.program_id(0)
    h = pl.program_id(1)
    slope = slopes_ref[h]
    q = _r16(q_ref[...])
    lane = lax.broadcasted_iota(jnp.int32, (1, LANES), 1)
    scores = []
    for r in range(MOBA_TOPK):
        for half in range(2):
            kpos = top_ref[b, h * MOBA_TOPK + r] * MOBA_BLOCK + half * LANES + lane
            s = jnp.sum(_r16(k_refs[2 * r + half][...]) * q, axis=0, keepdims=True)
            scores.append(s - slope * (past - kpos).astype(F32))
    s_new = jnp.sum(_r16(kn_ref[...]) * q, axis=0, keepdims=True)
    m = s_new
    for s in scores:
        m = jnp.maximum(m, jnp.max(s, axis=-1, keepdims=True))
    p_new = jnp.exp(s_new - m)
    probs = [jnp.exp(s - m) for s in scores]
    l = p_new
    for p in probs:
        l = l + jnp.sum(p, axis=-1, keepdims=True)
    o = _r16(p_new / l) * _r16(vn_ref[...])
    for p, v_ref in zip(probs, v_refs):
        o = o + jnp.sum(_r16(v_ref[...]) * _r16(p / l), axis=-1, keepdims=True)
    o_ref[...] = o


def _moba_sample(page_table, top, slopes, q, k_new, v_new, kpool_t, vpool_t):
    bsz, n_pages = page_table.shape
    past = n_pages * LANES

    def col_spec():
        return pl.BlockSpec((None, None, HEAD_DIM, 1), lambda b, h, pt, tp: (b, h, 0, 0))

    def page_spec(r, half):
        return pl.BlockSpec((None, None, HEAD_DIM, LANES),
                            lambda b, h, pt, tp: (pt[b, 2 * tp[b, h * MOBA_TOPK + r] + half], h, 0, 0))

    pages = [page_spec(r, half) for r in range(MOBA_TOPK) for half in range(2)]
    return pl.pallas_call(
        functools.partial(_moba_s_kernel, past=past),
        grid_spec=pltpu.PrefetchScalarGridSpec(
            num_scalar_prefetch=2,
            grid=(bsz, MOBA_HEADS),
            in_specs=[pl.BlockSpec(memory_space=pltpu.SMEM), col_spec(), col_spec(), col_spec()] + pages + pages,
            out_specs=col_spec(),
        ),
        out_shape=jax.ShapeDtypeStruct((bsz, MOBA_HEADS, HEAD_DIM, 1), F32),
        compiler_params=_cparams("parallel", "parallel"),
        name="moba_sample",
    )(page_table, top, slopes, q, k_new, v_new, *([kpool_t] * len(pages)), *([vpool_t] * len(pages)))


def _cmp_stage1_kernel(*refs, n_in):
    x_refs, w_ref, o_ref, x_scr = refs[:n_in], refs[n_in], refs[n_in + 1], refs[n_in + 2]
    st = NSA_CMP_STRIDE
    row = 0
    for r in x_refs:
        w = r.shape[-1]
        x_scr[row:row + w, :] = r[...].T
        row += w
    n_chunk = row // st
    xcat = jnp.concatenate([x_scr[pl.ds(s, n_chunk, stride=st), :] for s in range(st)], axis=1)
    o_ref[...] = jnp.dot(xcat.astype(BF16), w_ref[...], preferred_element_type=F32)


def _cmp_w1cat(w1):
    ratio = NSA_CMP_LEN // NSA_CMP_STRIDE
    w = w1.reshape(ratio, NSA_CMP_STRIDE * HEAD_DIM, NSA_CMP_HIDDEN)
    return jnp.concatenate([w[r] for r in range(ratio)], axis=1).astype(BF16)


def _cmp_stage1_dense(xt, row0, w1cat):
    bsz, _, seq = xt.shape
    n_chunk = seq // NSA_CMP_STRIDE
    return pl.pallas_call(
        functools.partial(_cmp_stage1_kernel, n_in=1),
        grid=(bsz, NSA_KV_HEADS),
        in_specs=[pl.BlockSpec((None, HEAD_DIM, seq), lambda b, g: (b, row0 // HEAD_DIM + g, 0)),
                  pl.BlockSpec(w1cat.shape, lambda b, g: (0, 0))],
        out_specs=pl.BlockSpec((None, None, n_chunk, LANES), lambda b, g: (b, g, 0, 0)),
        out_shape=jax.ShapeDtypeStruct((bsz, NSA_KV_HEADS, n_chunk, LANES), F32),
        scratch_shapes=[pltpu.VMEM((seq, HEAD_DIM), F32)],
        compiler_params=_cparams("parallel", "parallel"),
        name="cmp_stage1_dense",
    )(xt, w1cat)


CMP_PAGES_PER_STEP = 16


def _cmp_stage1_paged_kernel(pt_ref, *refs):
    pps, ng, st = CMP_PAGES_PER_STEP, NSA_KV_HEADS, NSA_CMP_STRIDE
    x_refs, w_ref, o_ref, x_scr = refs[:pps], refs[pps], refs[pps + 1], refs[pps + 2]
    for gp in range(ng // 2):
        for u, r in enumerate(x_refs):
            row = (gp * pps + u) * LANES
            x_scr[row:row + LANES, :] = jnp.concatenate([r[2 * gp], r[2 * gp + 1]], axis=0).T
    n_rows = (ng // 2) * pps * LANES // st
    xcat = jnp.concatenate([x_scr[pl.ds(s, n_rows, stride=st), :] for s in range(st)], axis=1)
    y = jnp.dot(xcat.astype(BF16), w_ref[...], preferred_element_type=F32)
    cps = pps * LANES // st
    for gp in range(ng // 2):
        for gl in range(2):
            o_ref[2 * gp + gl] = y[gp * cps:(gp + 1) * cps, gl * LANES:(gl + 1) * LANES]


def _cmp_stage1_paged(page_table, pool_t, w1cat):
    bsz, n_pages = page_table.shape
    pps = CMP_PAGES_PER_STEP
    assert n_pages % pps == 0 and NSA_KV_HEADS % 2 == 0
    cps = pps * LANES // NSA_CMP_STRIDE
    n_chunk = n_pages * LANES // NSA_CMP_STRIDE
    w3 = w1cat.reshape(NSA_CMP_STRIDE, 1, HEAD_DIM, 1, LANES)
    eye = jnp.eye(2, dtype=w1cat.dtype).reshape(1, 2, 1, 2, 1)
    w1cat = (w3 * eye).reshape(NSA_CMP_STRIDE * 2 * HEAD_DIM, 2 * LANES)

    def page_spec(u):
        return pl.BlockSpec((None, NSA_KV_HEADS, HEAD_DIM, LANES), lambda b, s, pt: (pt[b, s * pps + u], 0, 0, 0))

    return pl.pallas_call(
        _cmp_stage1_paged_kernel,
        grid_spec=pltpu.PrefetchScalarGridSpec(
            num_scalar_prefetch=1,
            grid=(bsz, n_pages // pps),
            in_specs=[page_spec(u) for u in range(pps)] + [pl.BlockSpec(w1cat.shape, lambda b, s, pt: (0, 0))],
            out_specs=pl.BlockSpec((None, NSA_KV_HEADS, cps, LANES), lambda b, s, pt: (b, 0, s, 0)),
            scratch_shapes=[pltpu.VMEM((NSA_KV_HEADS // 2 * pps * LANES, LANES), F32)],
        ),
        out_shape=jax.ShapeDtypeStruct((bsz, NSA_KV_HEADS, n_chunk, LANES), F32),
        compiler_params=_cparams("parallel", "arbitrary"),
        name="cmp_stage1_paged",
    )(page_table, *([pool_t] * pps), w1cat)


def _cmp_stage2_kernel(p_ref, pe_ref, w1_ref, b1_ref, w2_ref, b2_ref, gain_ref, o_ref, *, normalize):
    p = p_ref[...]
    n = p.shape[0]
    const = jnp.dot(pe_ref[...].astype(BF16), w1_ref[...].astype(BF16), preferred_element_type=F32) + b1_ref[...]
    shifted = pltpu.roll(pltpu.roll(p, n - 1, 0), NSA_CMP_HIDDEN, 1)
    hid = (p + shifted)[:, :NSA_CMP_HIDDEN] + const
    out = jnp.dot(_silu(hid).astype(BF16), w2_ref[...], preferred_element_type=F32) + b2_ref[...]
    if normalize:
        ms = jnp.mean(out * out, axis=-1, keepdims=True)
        out = out * lax.rsqrt(ms + RMS_EPS) * gain_ref[...]
    o_ref[...] = out.T


def _cmp_stage2(parts, pe, w1, b1, w2, b2, gain):
    bsz, ng, n_chunk, _ = parts.shape
    normalize = gain is not None
    gain = jnp.ones((HEAD_DIM,), F32) if gain is None else gain
    full = lambda a: pl.BlockSpec(a.shape, lambda b, g: (0,) * a.ndim)
    args = (pe.reshape(1, -1), w1, b1.reshape(1, -1), w2.astype(BF16), b2.reshape(1, -1), gain.reshape(1, -1))
    return pl.pallas_call(
        functools.partial(_cmp_stage2_kernel, normalize=normalize),
        grid=(bsz, ng),
        in_specs=[pl.BlockSpec((None, None, n_chunk, LANES), lambda b, g: (b, g, 0, 0))] + [full(a) for a in args],
        out_specs=pl.BlockSpec((None, None, HEAD_DIM, n_chunk), lambda b, g: (b, g, 0, 0)),
        out_shape=jax.ShapeDtypeStruct((bsz, ng, HEAD_DIM, n_chunk), F32),
        compiler_params=_cparams("parallel", "parallel"),
        name="cmp_stage2",
    )(parts, *args)


NSA_TQ = 128


def _nsa_p_kernel(slopes_ref, q_ref, kc_ref, vc_ref, ks_ref, vs_ref, kw_ref, vw_ref, gt_ref, gb_ref, ov_ref,
                  o_ref, sel_scr, s_slc, s_win, *, n_cmp, n_blk):
    g = pl.program_id(1)
    qi = pl.program_id(2)
    tq, grp, hd = NSA_TQ, NSA_GROUP, HEAD_DIM
    wq = grp * tq
    q4 = jnp.concatenate([q_ref[r * hd:(r + 1) * hd, :] for r in range(grp)], axis=1)
    qb = q4.astype(BF16)
    lane = lax.broadcasted_iota(jnp.int32, (1, wq), 1)
    qpos = qi * tq + lane % tq
    slope = jnp.zeros((1, wq), F32)
    for r in range(grp):
        slope = jnp.where(lane // tq == r, slopes_ref[g * grp + r], slope)

    def softmax_block(kt, vt, kpos, mask, carry):
        m, l, acc = carry
        s = lax.dot_general(kt, qb, _TN, preferred_element_type=F32)
        s = jnp.where(mask, s - slope * (qpos - kpos).astype(F32), NEG_INF)
        m_new = jnp.maximum(m, jnp.max(s, axis=0, keepdims=True))
        p = jnp.where(mask, jnp.exp(s - m_new), 0.0)
        alpha = jnp.exp(m - m_new)
        l = alpha * l + jnp.sum(p, axis=0, keepdims=True)
        acc = alpha * acc + jnp.dot(vt, p.astype(BF16), preferred_element_type=F32)
        return (m_new, l, acc), p

    def finish(carry):
        _, l, acc = carry
        return acc / jnp.where(l > 0.0, l, 1.0)

    init = (jnp.full((1, wq), NEG_INF, F32), jnp.zeros((1, wq), F32), jnp.zeros((hd, wq), F32))

    nc = kc_ref.shape[-1]
    ci = lax.broadcasted_iota(jnp.int32, (nc, 1), 0)
    c_end = ci * NSA_CMP_STRIDE + (NSA_CMP_LEN - 1)
    mask_c = (c_end <= qpos) & (ci < n_cmp)
    carry_c, p_c = softmax_block(kc_ref[...].astype(BF16), vc_ref[...].astype(BF16), c_end, mask_c, init)
    o_c = finish(carry_c)
    prob_c = p_c / jnp.where(carry_c[1] > 0.0, carry_c[1], 1.0)

    imp4 = jnp.dot(ov_ref[...].astype(BF16), prob_c.astype(BF16), preferred_element_type=F32)
    imp = imp4[:, 0:tq]
    for r in range(1, grp):
        imp = imp + imp4[:, r * tq:(r + 1) * tq]
    bi = lax.broadcasted_iota(jnp.int32, (n_blk, tq), 0)
    cur = (qi * tq + lax.broadcasted_iota(jnp.int32, (n_blk, tq), 1)) // NSA_SLC_BLOCK
    allowed = bi <= cur
    forced = (bi < NSA_INIT_BLOCKS) | (cur - bi < NSA_LOCAL_BLOCKS)
    val = jnp.where(allowed & forced, FORCE_SCORE, jnp.where(allowed, imp, NEG_INF))
    rank = jnp.zeros((n_blk, tq), F32)
    for j in range(n_blk):
        vj = val[j:j + 1, :]
        beats = (vj > val) | ((vj == val) & (j < bi))
        rank = rank + jnp.where(beats, 1.0, 0.0)
    big = -NEG_INF
    pen = jnp.where((rank < float(NSA_SLC_TOPN)) & allowed, 0.0, big)
    sel_scr[0:n_blk, :] = jnp.concatenate([pen] * grp, axis=1)

    half = NSA_SLC_BLOCK
    ksub = lax.broadcasted_iota(jnp.int32, (tq, 1), 0)
    rel = (lane % tq - ksub).astype(F32)
    alibi = slope * rel
    pen_diag = jnp.where(rel >= 0.0, alibi, big)
    pen_low = jnp.where(rel <= 0.0, alibi, big)

    def scores(k_ref_, scr, jj, row, penalty, parts, m):
        start = pl.multiple_of(jj * tq, tq)
        kt = k_ref_[:, pl.ds(start, tq)].astype(BF16)
        s = lax.dot_general(kt, qb, _TN, preferred_element_type=F32) - penalty
        for r0, r1, c in parts:
            sp = s[r0:r1] - c
            scr[pl.ds(pl.multiple_of(row + r0, half), r1 - r0), :] = sp
            m = jnp.maximum(m, jnp.max(sp, axis=0, keepdims=True))
        return m

    def accumulate(v_ref_, scr, jj, row, m, carry):
        l, acc = carry
        start = pl.multiple_of(jj * tq, tq)
        p = jnp.exp(scr[pl.ds(pl.multiple_of(row, tq), tq), :] - m)
        vt = v_ref_[:, pl.ds(start, tq)].astype(BF16)
        return (l + jnp.sum(p, axis=0, keepdims=True),
                acc + jnp.dot(vt, p.astype(BF16), preferred_element_type=F32))

    def slc_parts(jj, row_const):
        return [(0, half, row_const + sel_scr[pl.ds(2 * jj, 1), :]),
                (half, tq, row_const + sel_scr[pl.ds(2 * jj + 1, 1), :])]

    def tile_const(jj):
        return slope * ((qi - jj) * tq).astype(F32)

    m0 = jnp.full((1, wq), NEG_INF, F32)
    la0 = (jnp.zeros((1, wq), F32), jnp.zeros((hd, wq), F32))

    m = scores(ks_ref, s_slc, qi, qi * tq, pen_diag, slc_parts(qi, 0.0), m0)
    m = _loop_tiles(
        0, qi, lambda jj, mm: scores(ks_ref, s_slc, jj, jj * tq, alibi, slc_parts(jj, tile_const(jj)), mm), m)
    l, acc = _loop_tiles(0, qi + 1, lambda jj, c: accumulate(vs_ref, s_slc, jj, jj * tq, m, c), la0)
    o_s = acc / l

    n_wt = NSA_WINDOW // tq
    wrow = lambda jj: (jj - (qi - n_wt)) * tq
    first = jnp.maximum(qi - (n_wt - 1), 0)
    m = scores(kw_ref, s_win, qi, wrow(qi), pen_diag, [(0, tq, 0.0)], m0)
    m = _loop_tiles(
        first, qi, lambda jj, mm: scores(kw_ref, s_win, jj, wrow(jj), alibi, [(0, tq, tile_const(jj))], mm), m)
    has_low = jnp.where(qi >= n_wt, 1, 0)
    m = lax.fori_loop(
        0, has_low,
        lambda _, mm: scores(kw_ref, s_win, qi - n_wt, wrow(qi - n_wt), pen_low,
                             [(0, tq, tile_const(qi - n_wt))], mm), m)
    l, acc = _loop_tiles(first - has_low, qi + 1, lambda jj, c: accumulate(vw_ref, s_win, jj, wrow(jj), m, c), la0)
    o_w = acc / l

    outs = []
    for r in range(grp):
        acc = jnp.zeros((hd, tq), F32)
        for br, o_b in enumerate((o_c, o_s, o_w)):
            row = (g * grp + r) * 3 + br
            gt = gt_ref[pl.ds(row, 1), :] + gb_ref[pl.ds(row, 1), :]
            acc = acc + (1.0 / (1.0 + jnp.exp(-gt))) * o_b[:, r * tq:(r + 1) * tq]
        outs.append(acc)
    o_ref[...] = jnp.concatenate(outs, axis=0)


def _nsa_overlap_t(n_cmp_pad, n_blk):
    c_start = np.arange(n_cmp_pad) * NSA_CMP_STRIDE
    c_end = c_start + NSA_CMP_LEN - 1
    b_start = np.arange(n_blk) * NSA_SLC_BLOCK
    ov = (c_start[None, :] < b_start[:, None] + NSA_SLC_BLOCK) & (c_end[None, :] >= b_start[:, None])
    return jnp.asarray(ov.astype(np.float32))


def _nsa_prompt(yt, kcmp_t, vcmp_t, slopes, rows, gate_b_col):
    bsz, _, seq = yt.shape
    q0, ks0, vs0, kw0, vw0, g0 = rows
    n_chunk = kcmp_t.shape[-1]
    n_cmp = max((seq - NSA_CMP_LEN) // NSA_CMP_STRIDE + 1, 1)
    n_blk = max(-(-seq // NSA_SLC_BLOCK), NSA_SLC_TOPN)
    assert seq % NSA_TQ == 0 and n_blk * NSA_SLC_BLOCK == seq and n_blk % SUBLANES == 0
    assert NSA_TQ == 2 * NSA_SLC_BLOCK and NSA_WINDOW % NSA_TQ == 0 and NSA_LOCAL_BLOCKS >= 1
    hd, wq = HEAD_DIM, NSA_GROUP * NSA_TQ
    ov = _nsa_overlap_t(n_chunk, n_blk)

    def kv(row0):
        return pl.BlockSpec((None, hd, seq), lambda b, g, i: (b, row0 // hd + g, 0))

    cmp_spec = pl.BlockSpec((None, None, hd, n_chunk), lambda b, g, i: (b, g, 0, 0))
    return pl.pallas_call(
        functools.partial(_nsa_p_kernel, n_cmp=n_cmp, n_blk=n_blk),
        grid=(bsz, NSA_KV_HEADS, seq // NSA_TQ),
        in_specs=[
            pl.BlockSpec(memory_space=pltpu.SMEM),
            pl.BlockSpec((None, NSA_GROUP * hd, NSA_TQ), lambda b, g, i: (b, q0 // (NSA_GROUP * hd) + g, i)),
            cmp_spec, cmp_spec, kv(ks0), kv(vs0), kv(kw0), kv(vw0),
            pl.BlockSpec((None, LANES, NSA_TQ), lambda b, g, i: (b, g0 // LANES, i)),
            pl.BlockSpec((LANES, 1), lambda b, g, i: (0, 0)),
            pl.BlockSpec(ov.shape, lambda b, g, i: (0, 0)),
        ],
        out_specs=pl.BlockSpec((None, NSA_GROUP * hd, NSA_TQ), lambda b, g, i: (b, g, i)),
        out_shape=jax.ShapeDtypeStruct((bsz, NSA_HEADS * hd, seq), F32),
        scratch_shapes=[pltpu.VMEM((n_blk, wq), F32), pltpu.VMEM((seq, wq), F32),
                        pltpu.VMEM((NSA_WINDOW + NSA_TQ, wq), F32)],
        compiler_params=_cparams("parallel", "parallel", "arbitrary"),
        name="nsa_prompt",
    )(slopes, yt, kcmp_t, vcmp_t, yt, yt, yt, yt, yt, gate_b_col, ov)


def _nsa_s_cmp_kernel(slopes_ref, q_ref, kc_ref, vc_ref, ov_ref, oc_ref, sel_ref, *, past, n_cmp, n_blk):
    g = pl.program_id(1)
    grp = NSA_GROUP
    nc = kc_ref.shape[-1]
    nb = ov_ref.shape[0]
    ci = lax.broadcasted_iota(jnp.int32, (1, nc), 1)
    c_end = ci * NSA_CMP_STRIDE + (NSA_CMP_LEN - 1)
    mask = (c_end <= past) & (ci < n_cmp)
    dist = (past - c_end).astype(F32)
    kc = _r16(kc_ref[...])
    vc = _r16(vc_ref[...])
    probs = []
    for r in range(grp):
        s = jnp.sum(kc * _r16(q_ref[r]), axis=0, keepdims=True)
        s = jnp.where(mask, s - slopes_ref[g * grp + r] * dist, NEG_INF)
        m = jnp.max(s, axis=-1, keepdims=True)
        p = jnp.where(mask, jnp.exp(s - m), 0.0)
        l = jnp.sum(p, axis=-1, keepdims=True)
        p = _r16(p / jnp.where(l > 0.0, l, 1.0))
        probs.append(p)
        oc_ref[r] = jnp.sum(vc * p, axis=-1, keepdims=True)
    p8 = jnp.concatenate(probs + [jnp.zeros((SUBLANES - grp, nc), F32)], axis=0).astype(BF16)
    imp = jnp.sum(lax.dot_general(p8, ov_ref[...].astype(BF16), _NT, preferred_element_type=F32),
                  axis=0, keepdims=True)

    bi = lax.broadcasted_iota(jnp.int32, (1, nb), 1)
    cur = past // NSA_SLC_BLOCK
    allowed = (bi <= cur) & (bi < n_blk)
    forced = (bi < NSA_INIT_BLOCKS) | (cur - bi < NSA_LOCAL_BLOCKS)
    val = jnp.where(allowed & forced, FORCE_SCORE, jnp.where(allowed, imp, NEG_INF))
    val = jnp.where(bi < n_blk, val, -jnp.inf)
    bj = lax.broadcasted_iota(jnp.int32, (nb, 1), 0)
    val_col = jnp.sum(jnp.where(bj == bi, val, 0.0), axis=-1, keepdims=True)
    beats = (val_col > val) | ((val_col == val) & (bj < bi))
    rank = jnp.sum(jnp.where(beats, 1.0, 0.0), axis=0, keepdims=True)
    lane = lax.broadcasted_iota(jnp.int32, (1, LANES), 1)
    out = jnp.full((1, LANES), -1, jnp.int32)
    for k in range(NSA_SLC_TOPN):
        hit = (rank == float(k)) & allowed
        idx = jnp.sum(jnp.where(hit, bi, 0), axis=-1, keepdims=True)
        cnt = jnp.sum(jnp.where(hit, 1, 0), axis=-1, keepdims=True)
        out = jnp.where(lane == k, jnp.where(cnt > 0, idx, -1), out)
    sel_ref[...] = out


def _nsa_sample_cmp(q, kcmp_t, vcmp_t, slopes, past, n_keys):
    bsz = q.shape[0]
    n_chunk = kcmp_t.shape[-1]
    n_cmp = max((n_keys - NSA_CMP_LEN) // NSA_CMP_STRIDE + 1, 1)
    n_blk = max(-(-n_keys // NSA_SLC_BLOCK), NSA_SLC_TOPN)
    nb_pad = -(-n_blk // LANES) * LANES
    ov = _nsa_overlap_t(n_chunk, nb_pad)
    qspec = pl.BlockSpec((None, None, NSA_GROUP, HEAD_DIM, 1), lambda b, g: (b, g, 0, 0, 0))
    cspec = pl.BlockSpec((None, None, HEAD_DIM, n_chunk), lambda b, g: (b, g, 0, 0))
    return pl.pallas_call(
        functools.partial(_nsa_s_cmp_kernel, past=past, n_cmp=n_cmp, n_blk=n_blk),
        grid=(bsz, NSA_KV_HEADS),
        in_specs=[pl.BlockSpec(memory_space=pltpu.SMEM), qspec, cspec, cspec,
                  pl.BlockSpec(ov.shape, lambda b, g: (0, 0))],
        out_specs=[qspec, pl.BlockSpec((None, None, 1, LANES), lambda b, g: (b, g, 0, 0))],
        out_shape=[jax.ShapeDtypeStruct(q.shape, F32),
                   jax.ShapeDtypeStruct((bsz, NSA_KV_HEADS, 1, LANES), jnp.int32)],
        compiler_params=_cparams("parallel", "parallel"),
        name="nsa_sample_cmp",
    )(slopes, q, kcmp_t, vcmp_t, ov)


def _nsa_s_attn_kernel(pt_ref, sel_ref, slopes_ref, q_ref, oc_ref, gt_ref, gb_ref, ksn_ref, vsn_ref, kwn_ref,
                       vwn_ref, kw_ref, vw_ref, *refs, past):
    n_sel = NSA_SLC_TOPN
    k_refs, v_refs, o_ref = refs[:n_sel], refs[n_sel:2 * n_sel], refs[2 * n_sel]
    b = pl.program_id(0)
    g = pl.program_id(1)
    grp = NSA_GROUP
    pad = jnp.zeros((SUBLANES - grp, HEAD_DIM), F32)
    q8b = jnp.concatenate([q_ref[...], pad], axis=0).astype(BF16)
    q8r = q8b.astype(F32)
    sub = lax.broadcasted_iota(jnp.int32, (SUBLANES, 1), 0)
    slope = jnp.zeros((SUBLANES, 1), F32)
    for r in range(grp):
        slope = jnp.where(sub == r, slopes_ref[g * grp + r], slope)
    lane = lax.broadcasted_iota(jnp.int32, (1, LANES), 1)
    cur = past // NSA_SLC_BLOCK
    blks = [sel_ref[b, g * n_sel + k] for k in range(n_sel)]
    has_new = jnp.zeros((), jnp.int32)
    for blk in blks:
        has_new = jnp.maximum(has_new, (blk == cur).astype(jnp.int32))
    new_ok = jnp.full((1, 1), has_new, jnp.int32) > 0

    s_new = jnp.sum(q8r * _r16(ksn_ref[...]), axis=-1, keepdims=True)
    s_new = jnp.where(new_ok, s_new, NEG_INF)
    scores = []
    m_el = jnp.full((SUBLANES, LANES), NEG_INF, F32)
    for k in range(n_sel):
        blk = blks[k]
        kpos = (blk // 2) * LANES + lane
        valid = (kpos // NSA_SLC_BLOCK == blk) & (blk >= 0) & (kpos < past)
        s = jnp.dot(q8b, k_refs[k][...].astype(BF16), preferred_element_type=F32)
        s = jnp.where(valid, s - slope * (past - kpos).astype(F32), NEG_INF)
        scores.append((s, valid))
        m_el = jnp.maximum(m_el, s)
    m = jnp.maximum(jnp.max(m_el, axis=-1, keepdims=True), s_new)
    p_new = jnp.where(new_ok, jnp.exp(s_new - m), 0.0)
    probs = [jnp.where(valid, jnp.exp(s - m), 0.0) for s, valid in scores]
    l_el = probs[0]
    for p in probs[1:]:
        l_el = l_el + p
    l = jnp.sum(l_el, axis=-1, keepdims=True) + p_new
    inv_l = 1.0 / jnp.where(l > 0.0, l, 1.0)
    o_s = _r16(p_new * inv_l) * _r16(vsn_ref[...])
    for p, v_ref in zip(probs, v_refs):
        o_s = o_s + lax.dot_general((p * inv_l).astype(BF16), v_ref[...].astype(BF16), _NT,
                                    preferred_element_type=F32)

    n_win = kw_ref.shape[-1]
    wpos = past - n_win + lax.broadcasted_iota(jnp.int32, (1, n_win), 1)
    wdist = past - wpos
    wmask = (wpos >= 0) & (wdist <= NSA_WINDOW)
    sw = jnp.dot(q8b, kw_ref[...].astype(BF16), preferred_element_type=F32)
    sw = jnp.where(wmask, sw - slope * wdist.astype(F32), NEG_INF)
    sw_new = jnp.sum(q8r * _r16(kwn_ref[...]), axis=-1, keepdims=True)
    mw = jnp.maximum(jnp.max(sw, axis=-1, keepdims=True), sw_new)
    pw = jnp.where(wmask, jnp.exp(sw - mw), 0.0)
    pw_new = jnp.exp(sw_new - mw)
    inv_lw = 1.0 / (jnp.sum(pw, axis=-1, keepdims=True) + pw_new)
    o_w = (lax.dot_general((pw * inv_lw).astype(BF16), vw_ref[...].astype(BF16), _NT, preferred_element_type=F32)
           + _r16(pw_new * inv_lw) * _r16(vwn_ref[...]))

    gate = 1.0 / (1.0 + jnp.exp(-(gt_ref[...] + gb_ref[...])))
    gcol = []
    for br in range(3):
        col = jnp.zeros((SUBLANES, 1), F32)
        for r in range(grp):
            col = jnp.where(sub == r, gate[r * 3 + br:r * 3 + br + 1, :], col)
        gcol.append(col)
    o_c = jnp.concatenate([oc_ref[...], pad], axis=0)
    o_ref[...] = (gcol[0] * o_c + gcol[1] * o_s + gcol[2] * o_w)[0:grp]


def _nsa_sample_attn(page_table, sel, slopes, q, o_c, gt, gb, new_rows, kwin_t, vwin_t, kpool_t, vpool_t):
    bsz, n_pages = page_table.shape
    past = n_pages * LANES
    n_sel = NSA_SLC_TOPN
    n_win = kwin_t.shape[-1]
    qspec = pl.BlockSpec((None, None, NSA_GROUP, HEAD_DIM), lambda b, g, pt, sl: (b, g, 0, 0))
    col = pl.BlockSpec((None, None, 1, HEAD_DIM), lambda b, g, pt, sl: (b, g, 0, 0))
    win = pl.BlockSpec((None, None, HEAD_DIM, n_win), lambda b, g, pt, sl: (b, g, 0, 0))

    def page_spec(k):
        def index(b, g, pt, sl):
            page = jnp.clip(sl[b, g * n_sel + k] // 2, 0, n_pages - 1)
            return (pt[b, page], g, 0, 0)
        return pl.BlockSpec((None, None, HEAD_DIM, LANES), index)

    pages = [page_spec(k) for k in range(n_sel)]
    return pl.pallas_call(
        functools.partial(_nsa_s_attn_kernel, past=past),
        grid_spec=pltpu.PrefetchScalarGridSpec(
            num_scalar_prefetch=2,
            grid=(bsz, NSA_KV_HEADS),
            in_specs=[pl.BlockSpec(memory_space=pltpu.SMEM), qspec, qspec,
                      pl.BlockSpec((None, None, 3 * NSA_GROUP, 1), lambda b, g, pt, sl: (b, g, 0, 0)),
                      pl.BlockSpec((None, 3 * NSA_GROUP, 1), lambda b, g, pt, sl: (g, 0, 0)),
                      col, col, col, col, win, win] + pages + pages,
            out_specs=qspec,
        ),
        out_shape=jax.ShapeDtypeStruct(q.shape, F32),
        compiler_params=_cparams("parallel", "parallel"),
        name="nsa_sample_attn",
    )(page_table, sel, slopes, q, o_c, gt, gb, *new_rows, kwin_t, vwin_t,
      *([kpool_t] * n_sel), *([vpool_t] * n_sel))


MOE_TOKEN_TILE = 256
MOE_ROW_BLOCK = 512


def _router_kernel(x_ref, g_ref, rt_ref, h_ref, idx_ref, gate_ref):
    x = x_ref[...]
    ms = jnp.mean(x * x, axis=-1, keepdims=True)
    h = x * lax.rsqrt(ms + RMS_EPS) * g_ref[...]
    h_ref[...] = h
    logits = lax.dot_general(rt_ref[...].astype(BF16), h.astype(BF16), _NT, preferred_element_type=F32)
    e_id = lax.broadcasted_iota(jnp.int32, logits.shape, 0)
    v1 = jnp.max(logits, axis=0, keepdims=True)
    i1 = jnp.min(jnp.where(logits == v1, e_id, N_EXPERTS), axis=0, keepdims=True)
    rest = jnp.where(e_id == i1, -jnp.inf, logits)
    v2 = jnp.max(rest, axis=0, keepdims=True)
    i2 = jnp.min(jnp.where(rest == v2, e_id, N_EXPERTS), axis=0, keepdims=True)
    e2 = jnp.exp(v2 - v1)
    idx_ref[...] = jnp.concatenate([i1, i2], axis=0)
    gate_ref[...] = jnp.concatenate([1.0 / (1.0 + e2), e2 / (1.0 + e2)], axis=0)


def _router(x, g, router_t):
    n, d = x.shape
    tm = MOE_TOKEN_TILE
    return pl.pallas_call(
        _router_kernel,
        grid=(n // tm,),
        in_specs=[pl.BlockSpec((tm, d), lambda i: (i, 0)), pl.BlockSpec((1, d), lambda i: (0, 0)),
                  pl.BlockSpec(router_t.shape, lambda i: (0, 0))],
        out_specs=[pl.BlockSpec((tm, d), lambda i: (i, 0)), pl.BlockSpec((TOP_K, tm), lambda i: (0, i)),
                   pl.BlockSpec((TOP_K, tm), lambda i: (0, i))],
        out_shape=[jax.ShapeDtypeStruct((n, d), F32), jax.ShapeDtypeStruct((TOP_K, n), jnp.int32),
                   jax.ShapeDtypeStruct((TOP_K, n), F32)],
        compiler_params=_cparams("parallel"),
        name="moe_router",
    )(x, g, router_t)


def _gather_rows_kernel(idx_ref, src_ref, o_ref, sem):
    rows = o_ref.shape[0]

    def row_copy(r):
        return pltpu.make_async_copy(src_ref.at[pl.ds(idx_ref[0, r], 1)], o_ref.at[pl.ds(r, 1)], sem)

    def start(r, c):
        row_copy(r).start()
        return c

    def wait(r, c):
        row_copy(r).wait()
        return c

    lax.fori_loop(0, rows, start, 0, unroll=8)
    lax.fori_loop(0, rows, wait, 0, unroll=8)


def _gather_rows(src, idx, block):
    m = idx.shape[0]
    d = src.shape[1]
    return pl.pallas_call(
        _gather_rows_kernel,
        grid=(m // block,),
        in_specs=[pl.BlockSpec((None, 1, block), lambda i: (i, 0, 0), memory_space=pltpu.SMEM),
                  pl.BlockSpec(memory_space=pl.ANY)],
        out_specs=pl.BlockSpec((block, d), lambda i: (i, 0)),
        out_shape=jax.ShapeDtypeStruct((m, d), src.dtype),
        scratch_shapes=[pltpu.SemaphoreType.DMA(())],
        compiler_params=_cparams("arbitrary"),
        name="gather_rows",
    )(idx.reshape(m // block, 1, block), src)


def _experts_kernel(be_ref, act_ref, x_ref, wg_ref, wu_ref, wd_ref, o_ref, acc_scr):
    i = pl.program_id(0)
    j = pl.program_id(1)

    @pl.when(act_ref[i] > 0)
    def _():
        @pl.when(j == 0)
        def _():
            acc_scr[...] = jnp.zeros_like(acc_scr)

        xb = x_ref[...].astype(BF16)
        gate = jnp.dot(xb, wg_ref[...], preferred_element_type=F32)
        up = jnp.dot(xb, wu_ref[...], preferred_element_type=F32)
        acc_scr[...] += jnp.dot((_silu(gate) * up).astype(BF16), wd_ref[...], preferred_element_type=F32)

    @pl.when(j == pl.num_programs(1) - 1)
    def _():
        o_ref[...] = jnp.where(act_ref[i] > 0, acc_scr[...], 0.0)


def _experts(block_expert, block_active, xg, wg, wu, wd):
    m, d = xg.shape
    rb = MOE_ROW_BLOCK
    f = wg.shape[2]
    tf = _largest_tile(f, 1792, LANES)
    return pl.pallas_call(
        _experts_kernel,
        grid_spec=pltpu.PrefetchScalarGridSpec(
            num_scalar_prefetch=2,
            grid=(m // rb, f // tf),
            in_specs=[
                pl.BlockSpec((rb, d), lambda i, j, be, act: (i, 0)),
                pl.BlockSpec((None, d, tf), lambda i, j, be, act: (be[i], 0, j)),
                pl.BlockSpec((None, d, tf), lambda i, j, be, act: (be[i], 0, j)),
                pl.BlockSpec((None, tf, d), lambda i, j, be, act: (be[i], j, 0)),
            ],
            out_specs=pl.BlockSpec((rb, d), lambda i, j, be, act: (i, 0)),
            scratch_shapes=[pltpu.VMEM((rb, d), F32)],
        ),
        out_shape=jax.ShapeDtypeStruct((m, d), F32),
        compiler_params=_cparams("arbitrary", "arbitrary"),
        name="moe_experts",
    )(block_expert, block_active, xg, wg, wu, wd)


def _combine_kernel(idx_ref, x_ref, yg_ref, g0_ref, g1_ref, o_ref, y_scr, sem):
    tm = x_ref.shape[0]

    def row_copy(r):
        return pltpu.make_async_copy(yg_ref.at[pl.ds(idx_ref[0, r], 1)], y_scr.at[pl.ds(r, 1)], sem)

    def start(r, c):
        row_copy(r).start()
        return c

    def wait(r, c):
        row_copy(r).wait()
        return c

    lax.fori_loop(0, TOP_K * tm, start, 0, unroll=8)
    lax.fori_loop(0, TOP_K * tm, wait, 0, unroll=8)
    o_ref[...] = x_ref[...] + (g0_ref[...] * y_scr[0:tm, :] + g1_ref[...] * y_scr[tm:2 * tm, :])


def _combine(x, yg, slot, gates):
    n, d = x.shape
    tm = MOE_TOKEN_TILE
    nt = n // tm
    assert TOP_K == 2
    g0 = gates[0].reshape(n, 1)
    g1 = gates[1].reshape(n, 1)
    idx = jnp.transpose(slot.reshape(TOP_K, nt, tm), (1, 0, 2)).reshape(nt, 1, TOP_K * tm)
    return pl.pallas_call(
        _combine_kernel,
        grid=(nt,),
        in_specs=[pl.BlockSpec((None, 1, TOP_K * tm), lambda i: (i, 0, 0), memory_space=pltpu.SMEM),
                  pl.BlockSpec((tm, d), lambda i: (i, 0)), pl.BlockSpec(memory_space=pl.ANY),
                  pl.BlockSpec((tm, 1), lambda i: (i, 0)), pl.BlockSpec((tm, 1), lambda i: (i, 0))],
        out_specs=pl.BlockSpec((tm, d), lambda i: (i, 0)),
        out_shape=jax.ShapeDtypeStruct((n, d), F32),
        scratch_shapes=[pltpu.VMEM((TOP_K * tm, d), F32), pltpu.SemaphoreType.DMA(())],
        compiler_params=_cparams("arbitrary"),
        name="moe_combine",
    )(idx, x, yg, g0, g1)


def _moe(x, norm, router_t, wg, wu, wd):
    n = x.shape[0]
    rb = MOE_ROW_BLOCK
    h, idx, gates = _router(x, norm.reshape(1, -1), router_t)
    expert = idx.reshape(-1)
    n_asg = expert.shape[0]
    onehot = (expert[:, None] == jnp.arange(N_EXPERTS, dtype=jnp.int32)[None, :]).astype(jnp.int32)
    pos = jnp.take_along_axis(jnp.cumsum(onehot, axis=0), expert[:, None], axis=1)[:, 0] - 1
    counts = jnp.sum(onehot, axis=0)
    padded = (counts + rb - 1) // rb * rb
    p_end = jnp.cumsum(padded)
    slot = (p_end - padded)[expert] + pos
    n_blocks = -(-n_asg // rb) + N_EXPERTS
    token = jnp.tile(jnp.arange(n, dtype=jnp.int32), TOP_K)
    tok_of_slot = jnp.zeros((n_blocks * rb,), jnp.int32).at[slot].set(token)
    starts = jnp.arange(n_blocks, dtype=jnp.int32) * rb
    block_expert = jnp.minimum(jnp.searchsorted(p_end, starts, side="right"), N_EXPERTS - 1).astype(jnp.int32)
    block_active = (starts < p_end[-1]).astype(jnp.int32)
    xg = _gather_rows(h, tok_of_slot, rb)
    yg = _experts(block_expert, block_active, xg, wg, wu, wd)
    return _combine(x, yg, slot.astype(jnp.int32), gates)


def _col(v):
    return v.reshape(-1, 1).astype(F32)


def _gm_weights(w_in, q_norm, k_norm):
    wt = w_in.T
    n_ga0 = 2 * GLA_HEADS * GLA_DK + GLA_HEADS * GLA_DV
    wt = jnp.concatenate([wt[:n_ga0], wt[n_ga0 + GLA_GATE_RANK:], wt[n_ga0:n_ga0 + GLA_GATE_RANK],
                          jnp.zeros((LANES - GLA_GATE_RANK, wt.shape[1]), wt.dtype)], axis=0).astype(BF16)
    ones = lambda n: jnp.ones((n,), F32)
    nm = MOBA_HEADS * HEAD_DIM
    gain = jnp.concatenate([ones(256) * GLA_DK ** -0.5, ones(256), ones(512), ones(512),
                            jnp.tile(q_norm, MOBA_HEADS) * HEAD_DIM ** -0.5, jnp.tile(k_norm, MOBA_HEADS),
                            ones(nm), ones(LANES)])
    nflag = jnp.concatenate([jnp.zeros((1536,), F32), ones(2 * nm), jnp.zeros((nm + LANES,), F32)])
    return wt, _col(nflag), _col(gain)


GM_ROWS = dict(qg=0, kg=256, vg=512, rg=1024, qm=1536, km=2048, vm=2560, ga=3072)


def _gm_prompt(xp, bsz, seq, norm_mix, wts, gate_w, gate_b, out_norm, w_out, tm):
    wt, nflag, gain = wts
    r = GM_ROWS
    yt = _proj_t(xp, 0, bsz, seq, norm_mix.reshape(1, -1), wt, nflag, gain, _largest_tile(seq, WIDE_TOKEN_TILE, LANES))
    gw_t = gate_w.T.reshape(GLA_HEADS, GLA_DK, GLA_GATE_RANK)
    gb = gate_b.reshape(GLA_HEADS, GLA_DK, 1)
    on = out_norm.reshape(GLA_HEADS, GLA_DV, 1)
    og, s_t = _gla_prompt(yt, (r["qg"], r["kg"], r["vg"], r["rg"], r["ga"]), gw_t, gb, on)
    om = _moba_prompt(yt, _alibi_slopes(MOBA_HEADS), r["qm"], r["km"], r["vm"])
    xp = _outproj_t(xp, 0, [og, om], w_out.astype(BF16), tm)
    return xp, s_t, yt


NS_ROWS = dict(q=0, kc=1024, vc=1280, ks=1536, vs=1792, kw=2048, vw=2304, g=2560)


def _ns_weights(w_in, q_norm, ks_norm, kw_norm):
    wt = w_in.T
    n_g = NSA_HEADS * 3
    wt = jnp.concatenate([wt, jnp.zeros((LANES - n_g, wt.shape[1]), wt.dtype)], axis=0).astype(BF16)
    ones = lambda n: jnp.ones((n,), F32)
    zeros = lambda n: jnp.zeros((n,), F32)
    kvw = NSA_KV_HEADS * HEAD_DIM
    gain = jnp.concatenate([jnp.tile(q_norm, NSA_HEADS) * HEAD_DIM ** -0.5, ones(2 * kvw),
                            jnp.tile(ks_norm, NSA_KV_HEADS), ones(kvw),
                            jnp.tile(kw_norm, NSA_KV_HEADS), ones(kvw), ones(LANES)])
    nflag = jnp.concatenate([ones(NSA_HEADS * HEAD_DIM), zeros(2 * kvw), ones(kvw), zeros(kvw), ones(kvw),
                             zeros(kvw), zeros(LANES)])
    return wt, _col(nflag), _col(gain)


def _ns_prompt(xp, bsz, seq, norm_mix, wts, gate_b, kc_norm, cmpk, cmpv, w_out, tm):
    wt, nflag, gain = wts
    r = NS_ROWS
    yt = _proj_t(xp, 0, bsz, seq, norm_mix.reshape(1, -1), wt, nflag, gain, _largest_tile(seq, WIDE_TOKEN_TILE, LANES))
    kcmp_t =_cmp_stage2(_cmp_stage1_dense(yt, r["kc"], _cmp_w1cat(cmpk[1])), *cmpk, kc_norm)
    vcmp_t = _cmp_stage2(_cmp_stage1_dense(yt, r["vc"], _cmp_w1cat(cmpv[1])), *cmpv, None)
    gb = jnp.pad(gate_b, (0, LANES - gate_b.shape[0])).reshape(LANES, 1)
    o_t = _nsa_prompt(yt, kcmp_t, vcmp_t, _alibi_slopes(NSA_HEADS),
                      (r["q"], r["ks"], r["vs"], r["kw"], r["vw"], r["g"]), gb)
    xp = _outproj_t(xp, 0, [o_t], w_out.astype(BF16), tm)
    return xp, yt


def _sample_rows(xs, norm, wts):
    n = xs.shape[0]
    wt, nflag, gain = wts
    xs_pad = jnp.pad(xs, ((0, LANES - n), (0, 0)))
    yt = _proj_t(xs_pad, 0, 1, LANES, norm.reshape(1, -1), wt, nflag, gain, LANES)
    return yt[0, :, :n].T


def _ns_sample(xs, norm_mix, wts, gate_b, kc_norm, cmpk, cmpv, w_out, pools, win_k, win_v, page_table):
    n = xs.shape[0]
    r = NS_ROWS
    past = page_table.shape[1] * LANES
    ys = _sample_rows(xs, norm_mix, wts)
    kvw = NSA_KV_HEADS * HEAD_DIM
    new = {nm: ys[:, r[nm]:r[nm] + kvw].reshape(n, NSA_KV_HEADS, HEAD_DIM) for nm in ("kc", "vc", "ks", "vs", "kw", "vw")}
    colv = lambda a: a.reshape(n, NSA_KV_HEADS, HEAD_DIM, 1)
    q = ys[:, :NSA_HEADS * HEAD_DIM].reshape(n, NSA_KV_HEADS, NSA_GROUP, HEAD_DIM, 1)
    gt = ys[:, r["g"]:r["g"] + 3 * NSA_HEADS].reshape(n, NSA_KV_HEADS, 3 * NSA_GROUP, 1)
    gb = gate_b.reshape(NSA_KV_HEADS, 3 * NSA_GROUP, 1)
    view = lambda a: jnp.transpose(a, (0, 2, 3, 1))
    ck_t, cv_t, sk_t, sv_t = (view(p) for p in pools)
    kcmp_t = _cmp_stage2(_cmp_stage1_paged(page_table, ck_t, _cmp_w1cat(cmpk[1])), *cmpk, kc_norm)
    vcmp_t = _cmp_stage2(_cmp_stage1_paged(page_table, cv_t, _cmp_w1cat(cmpv[1])), *cmpv, None)
    slopes = _alibi_slopes(NSA_HEADS)
    o_c, sel = _nsa_sample_cmp(q, kcmp_t, vcmp_t, slopes, past, past + 1)
    sel = sel[:, :, 0, :NSA_SLC_TOPN].reshape(n, NSA_KV_HEADS * NSA_SLC_TOPN)
    rows = lambda a: a.reshape(n, NSA_KV_HEADS, -1, HEAD_DIM)
    o = _nsa_sample_attn(page_table, sel, slopes, rows(q), rows(o_c), gt, gb,
                         [rows(new[nm]) for nm in ("ks", "vs", "kw", "vw")], view(win_k), view(win_v), sk_t, sv_t)
    xs = _outproj_r(xs, o.reshape(n, NSA_HEADS * HEAD_DIM), w_out.astype(BF16))
    return xs, new


def _gm_sample(xs, norm_mix, wts, gate_w, gate_b, out_norm, w_out, state, pool_k, pool_v, page_table):
    n = xs.shape[0]
    r = GM_ROWS
    ys = _sample_rows(xs, norm_mix, wts)
    hk, hv, nm = GLA_HEADS * GLA_DK, GLA_HEADS * GLA_DV, MOBA_HEADS * HEAD_DIM
    col = lambda r0, heads, dim: ys[:, r0:r0 + heads * dim].reshape(n, heads, dim, 1)
    row = lambda r0, heads, dim: ys[:, r0:r0 + heads * dim].reshape(n, heads, 1, dim)
    gw_t = gate_w.T.reshape(GLA_HEADS, GLA_DK, GLA_GATE_RANK)
    og, s_new = _gla_sample(
        col(r["qg"], GLA_HEADS, GLA_DK), col(r["kg"], GLA_HEADS, GLA_DK), row(r["vg"], GLA_HEADS, GLA_DV),
        row(r["rg"], GLA_HEADS, GLA_DV), ys[:, r["ga"]:r["ga"] + GLA_GATE_RANK].reshape(n, 1, 1, GLA_GATE_RANK),
        gw_t, gate_b.reshape(GLA_HEADS, GLA_DK, 1), out_norm.reshape(GLA_HEADS, 1, GLA_DV), state)
    qm, km, vm = (col(r[nme], MOBA_HEADS, HEAD_DIM) for nme in ("qm", "km", "vm"))
    kpool_t = jnp.transpose(pool_k, (0, 2, 3, 1))
    vpool_t = jnp.transpose(pool_v, (0, 2, 3, 1))
    _, top = _moba_gates(page_table, qm, kpool_t)
    top = top[:, :, :MOBA_TOPK].reshape(n, MOBA_HEADS * MOBA_TOPK)
    om = _moba_sample(page_table, top, _alibi_slopes(MOBA_HEADS), qm, km, vm, kpool_t, vpool_t)
    a = jnp.concatenate([og.reshape(n, hv), om.reshape(n, nm)], axis=1)
    xs = _outproj_r(xs, a, w_out.astype(BF16))
    return xs, s_new, km.reshape(n, MOBA_HEADS, HEAD_DIM), vm.reshape(n, MOBA_HEADS, HEAD_DIM)


def _rows_to_cache(yt, row0, heads):
    bsz, _, seq = yt.shape
    a = yt[:, row0:row0 + heads * HEAD_DIM].reshape(bsz, heads, HEAD_DIM, seq)
    return jnp.transpose(a, (0, 3, 1, 2))


def kernel(x_prompt, x_sample, state_gla, cache_moba_k, cache_moba_v, cache_nsa_cmp_k, cache_nsa_cmp_v, cache_nsa_slc_k, cache_nsa_slc_v, state_nsa_win_k, state_nsa_win_v, page_table, gm_norm_mix, gm_w_in, gm_gla_gate_w, gm_gla_gate_b, gm_gla_out_norm, gm_moba_q_norm, gm_moba_k_norm, gm_w_out, gm_norm_ffn, gm_ffn_gate, gm_ffn_up, gm_ffn_down, ns_norm_mix, ns_w_in, ns_gate_b, ns_q_norm, ns_kcmp_norm, ns_kslc_norm, ns_kwin_norm, ns_cmpk_pe, ns_cmpk_w1, ns_cmpk_b1, ns_cmpk_w2, ns_cmpk_b2, ns_cmpv_pe, ns_cmpv_w1, ns_cmpv_b1, ns_cmpv_w2, ns_cmpv_b2, ns_w_out, ns_norm_ffn, ns_router, ns_exp_gate, ns_exp_up, ns_exp_down):
    bsz, seq, d = x_prompt.shape
    n_s = x_sample.shape[0]
    assert x_sample.shape[1] == 1 and gm_w_in.shape[0] == 1 and ns_w_in.shape[0] == 1
    tm = _largest_tile(seq, 512, LANES)
    xp = x_prompt.reshape(bsz * seq, d)
    xs = x_sample.reshape(n_s, d)

    i = 0
    wts = _gm_weights(gm_w_in[i], gm_moba_q_norm[i], gm_moba_k_norm[i])
    xp, gla_p, yt0 = _gm_prompt(xp, bsz, seq, gm_norm_mix[i], wts, gm_gla_gate_w[i], gm_gla_gate_b[i],
                                gm_gla_out_norm[i], gm_w_out[i], tm)
    xs, gla_s, mk_s, mv_s = _gm_sample(xs, gm_norm_mix[i], wts, gm_gla_gate_w[i], gm_gla_gate_b[i],
                                       gm_gla_out_norm[i], gm_w_out[i], state_gla[i], cache_moba_k[i],
                                       cache_moba_v[i], page_table)
    ffn = (gm_norm_ffn[i].reshape(1, -1), gm_ffn_gate[i].astype(BF16), gm_ffn_up[i].astype(BF16),
           gm_ffn_down[i].astype(BF16))
    xp = _ffn(xp, *ffn, _largest_tile(seq, WIDE_TOKEN_TILE, LANES))
    xs = _ffn(xs, *ffn, n_s)

    wts = _ns_weights(ns_w_in[i], ns_q_norm[i], ns_kslc_norm[i], ns_kwin_norm[i])
    cmpk = (ns_cmpk_pe[i], ns_cmpk_w1[i], ns_cmpk_b1[i], ns_cmpk_w2[i], ns_cmpk_b2[i])
    cmpv = (ns_cmpv_pe[i], ns_cmpv_w1[i], ns_cmpv_b1[i], ns_cmpv_w2[i], ns_cmpv_b2[i])
    xp, yt1 = _ns_prompt(xp, bsz, seq, ns_norm_mix[i], wts, ns_gate_b[i], ns_kcmp_norm[i], cmpk, cmpv,
                         ns_w_out[i], tm)
    pools = (cache_nsa_cmp_k[i], cache_nsa_cmp_v[i], cache_nsa_slc_k[i], cache_nsa_slc_v[i])
    xs, new = _ns_sample(xs, ns_norm_mix[i], wts, ns_gate_b[i], ns_kcmp_norm[i], cmpk, cmpv, ns_w_out[i], pools,
                         state_nsa_win_k[i], state_nsa_win_v[i], page_table)
    n_tok = bsz * seq + n_s
    n_pad = -(-n_tok // MOE_TOKEN_TILE) * MOE_TOKEN_TILE
    x_all = jnp.concatenate([xp, xs, jnp.zeros((n_pad - n_tok, d), F32)], axis=0)
    x_all = _moe(x_all, ns_norm_ffn[i], ns_router[i].T, ns_exp_gate[i].astype(BF16), ns_exp_up[i].astype(BF16),
                 ns_exp_down[i].astype(BF16))
    y_prompt = x_all[:bsz * seq].reshape(bsz, seq, d)
    y_sample = x_all[bsz * seq:n_tok].reshape(n_s, 1, d)

    r0, r1 = GM_ROWS, NS_ROWS
    n_win = min(NSA_WINDOW, seq)
    ns_p = {nm: _rows_to_cache(yt1, r1[nm], NSA_KV_HEADS) for nm in ("kc", "vc", "ks", "vs", "kw", "vw")}
    win_s = lambda state, row: jnp.concatenate([state, row[:, None]], axis=1)[:, -min(NSA_WINDOW, state.shape[1] + 1):]
    lead = lambda a: a[None]
    tok = lambda a: a[None, :, None]
    return (y_prompt, y_sample, lead(gla_p), lead(gla_s),
            lead(_rows_to_cache(yt0, r0["km"], MOBA_HEADS)), lead(_rows_to_cache(yt0, r0["vm"], MOBA_HEADS)),
            tok(mk_s), tok(mv_s),
            lead(ns_p["kc"]), lead(ns_p["vc"]), lead(ns_p["ks"]), lead(ns_p["vs"]),
            lead(ns_p["kw"][:, -n_win:]), lead(ns_p["vw"][:, -n_win:]),
            tok(new["kc"]), tok(new["vc"]), tok(new["ks"]), tok(new["vs"]),
            lead(win_s(state_nsa_win_k[i], new["kw"])), lead(win_s(state_nsa_win_v[i], new["vw"])))
```

```python
import functools

import numpy as np
import jax
import jax.numpy as jnp
from jax import lax
from jax.experimental import pallas as pl
from jax.experimental.pallas import tpu as pltpu

F32 = jnp.float32
BF16 = jnp.bfloat16

HEAD_DIM = 64
GLA_HEADS, GLA_DK, GLA_DV, GLA_GATE_RANK, GLA_TAU, GLA_CHUNK = 4, 64, 128, 16, 16.0, 16
MOBA_HEADS, MOBA_BLOCK, MOBA_TOPK = 8, 256, 3
NSA_HEADS, NSA_KV_HEADS = 16, 4
NSA_GROUP = NSA_HEADS // NSA_KV_HEADS
NSA_CMP_LEN, NSA_CMP_STRIDE, NSA_CMP_HIDDEN = 32, 16, 64
NSA_SLC_BLOCK, NSA_SLC_TOPN, NSA_INIT_BLOCKS, NSA_LOCAL_BLOCKS, NSA_WINDOW = 64, 16, 1, 2, 512
N_EXPERTS, TOP_K = 8, 2
QUERY_BLOCK = 16
RMS_EPS = 1e-6
NEG_INF = -1e30
FORCE_SCORE = 1e9

LANES = 128
SUBLANES = 8
VMEM_LIMIT_BYTES = 56 * 1024 * 1024
WIDE_TOKEN_TILE = 1024

_NT = (((1,), (1,)), ((), ()))
_TN = (((0,), (0,)), ((), ()))
_HI = lax.Precision.HIGHEST


def _cparams(*sem):
    return pltpu.CompilerParams(dimension_semantics=sem, vmem_limit_bytes=VMEM_LIMIT_BYTES)


def _largest_tile(n, cap, mult):
    t = (cap // mult) * mult
    while t > mult and n % t:
        t -= mult
    assert n % t == 0, (n, cap, mult)
    return t


def _alibi_slopes(n):
    return 2.0 ** (-8.0 * jnp.arange(1, n + 1, dtype=F32) / n)


def _silu(x):
    return x * (1.0 / (1.0 + jnp.exp(-x)))


def _loop_tiles(lo, hi, fn, carry, groups=(4, 2, 1)):
    for group in groups:
        n = (hi - lo) // group

        def trip(i, c, lo=lo, group=group):
            for u in range(group):
                c = fn(lo + group * i + u, c)
            return c

        carry = lax.fori_loop(0, n, trip, carry)
        lo = lo + n * group
    return carry


def _r16(x):
    return x.astype(BF16).astype(F32)


def _proj_t_kernel(x_ref, g_ref, wt_ref, nflag_ref, gain_ref, o_ref, h_scr):
    @pl.when(pl.program_id(1) == 0)
    def _():
        x = x_ref[...]
        ms = jnp.mean(x * x, axis=-1, keepdims=True)
        h_scr[...] = (x * lax.rsqrt(ms + RMS_EPS) * g_ref[...]).astype(BF16)

    y = lax.dot_general(wt_ref[...], h_scr[...], _NT, preferred_element_type=F32)
    tc, tm = y.shape
    y3 = y.reshape(tc // HEAD_DIM, HEAD_DIM, tm)
    ms = jnp.mean(y3 * y3, axis=1, keepdims=True)
    nf = nflag_ref[...].reshape(tc // HEAD_DIM, HEAD_DIM, 1)
    gn = gain_ref[...].reshape(tc // HEAD_DIM, HEAD_DIM, 1)
    scale = jnp.where(nf > 0.0, lax.rsqrt(ms + RMS_EPS), 1.0) * gn
    o_ref[...] = (y3 * scale).reshape(tc, tm)


def _proj_t(x, row0, bsz, seq, g, wt, nflag, gain, tm):
    d = x.shape[1]
    c = wt.shape[0]
    tc = _largest_tile(c, 640, LANES)
    nt = seq // tm
    return pl.pallas_call(
        _proj_t_kernel,
        grid=(bsz * nt, c // tc),
        in_specs=[
            pl.BlockSpec((tm, d), lambda i, j: (row0 // tm + i, 0)),
            pl.BlockSpec((1, d), lambda i, j: (0, 0)),
            pl.BlockSpec((tc, d), lambda i, j: (j, 0)),
            pl.BlockSpec((tc, 1), lambda i, j: (j, 0)),
            pl.BlockSpec((tc, 1), lambda i, j: (j, 0)),
        ],
        out_specs=pl.BlockSpec((None, tc, tm), lambda i, j: (i // nt, j, i % nt)),
        out_shape=jax.ShapeDtypeStruct((bsz, c, seq), F32),
        scratch_shapes=[pltpu.VMEM((tm, d), BF16)],
        compiler_params=_cparams("parallel", "arbitrary"),
        name="proj_t",
    )(x, g, wt, nflag, gain)


def _outproj_t_kernel(*refs, n_in):
    x_ref, a_refs, w_ref, o_ref = refs[0], refs[1:1 + n_in], refs[1 + n_in], refs[2 + n_in]
    a = jnp.concatenate([r[...].astype(BF16) for r in a_refs], axis=0)
    o_ref[...] = x_ref[...] + lax.dot_general(a, w_ref[...], _TN, preferred_element_type=F32)


def _outproj_t(x, row0, a_list, w, tm):
    d = x.shape[1]
    bsz, _, seq = a_list[0].shape
    nt = seq // tm
    a_specs = [pl.BlockSpec((None, a.shape[1], tm), lambda i: (i // nt, 0, i % nt)) for a in a_list]
    return pl.pallas_call(
        functools.partial(_outproj_t_kernel, n_in=len(a_list)),
        grid=(bsz * nt,),
        in_specs=[pl.BlockSpec((tm, d), lambda i: (row0 // tm + i, 0))] + a_specs
        + [pl.BlockSpec(w.shape, lambda i: (0, 0))],
        out_specs=pl.BlockSpec((tm, d), lambda i: (i, 0)),
        out_shape=jax.ShapeDtypeStruct((bsz * seq, d), F32),
        compiler_params=_cparams("parallel"),
        name="outproj_t",
    )(x, *a_list, w)


def _outproj_r_kernel(x_ref, a_ref, w_ref, o_ref):
    o_ref[...] = x_ref[...] + jnp.dot(a_ref[...].astype(BF16), w_ref[...], preferred_element_type=F32)


def _outproj_r(x, a, w):
    return pl.pallas_call(
        _outproj_r_kernel,
        out_shape=jax.ShapeDtypeStruct(x.shape, F32),
        compiler_params=pltpu.CompilerParams(vmem_limit_bytes=VMEM_LIMIT_BYTES),
        name="outproj_r",
    )(x, a, w)


def _ffn_kernel(x_ref, g_ref, wg_ref, wu_ref, wd_ref, o_ref, h_scr, acc_scr):
    j = pl.program_id(1)

    @pl.when(j == 0)
    def _():
        x = x_ref[...]
        ms = jnp.mean(x * x, axis=-1, keepdims=True)
        h_scr[...] = (x * lax.rsqrt(ms + RMS_EPS) * g_ref[...]).astype(BF16)
        acc_scr[...] = jnp.zeros_like(acc_scr)

    h = h_scr[...]
    gate = jnp.dot(h, wg_ref[...], preferred_element_type=F32)
    up = jnp.dot(h, wu_ref[...], preferred_element_type=F32)
    act = (_silu(gate) * up).astype(BF16)
    acc_scr[...] += jnp.dot(act, wd_ref[...], preferred_element_type=F32)

    @pl.when(j == pl.num_programs(1) - 1)
    def _():
        o_ref[...] = x_ref[...] + acc_scr[...]


def _ffn(x, g, wg, wu, wd, tm):
    n, d = x.shape
    f = wg.shape[1]
    tf = _largest_tile(f, 1536, LANES)
    return pl.pallas_call(
        _ffn_kernel,
        grid=(n // tm, f // tf),
        in_specs=[
            pl.BlockSpec((tm, d), lambda i, j: (i, 0)),
            pl.BlockSpec((1, d), lambda i, j: (0, 0)),
            pl.BlockSpec((d, tf), lambda i, j: (0, j)),
            pl.BlockSpec((d, tf), lambda i, j: (0, j)),
            pl.BlockSpec((tf, d), lambda i, j: (j, 0)),
        ],
        out_specs=pl.BlockSpec((tm, d), lambda i, j: (i, 0)),
        out_shape=jax.ShapeDtypeStruct((n, d), F32),
        scratch_shapes=[pltpu.VMEM((tm, d), BF16), pltpu.VMEM((tm, d), F32)],
        compiler_params=_cparams("parallel", "arbitrary"),
        name="ffn",
    )(x, g, wg, wu, wd)


def _moba_p_kernel(slopes_ref, q_ref, k_ref, v_ref, o_ref, sel_scr, s_scr, km_scr, *, n_blk):
    h = pl.program_id(1)
    qi = pl.program_id(2)
    blk = MOBA_BLOCK
    slope = slopes_ref[h]
    q = q_ref[...]
    qb = q.astype(BF16)

    @pl.when(qi == 0)
    def _():
        for j in range(n_blk):
            km_scr[:, j:j + 1] = _r16(jnp.mean(k_ref[:, j * blk:(j + 1) * blk], axis=1, keepdims=True))

    gates = []
    for j in range(n_blk):
        gates.append(jnp.sum(qb.astype(F32) * km_scr[:, j:j + 1], axis=0, keepdims=True))
    for j in range(n_blk):
        cnt = jnp.zeros_like(gates[0])
        for j2 in range(n_blk):
            if j2 == j:
                continue
            beats = (gates[j2] >= gates[j]) if j2 < j else (gates[j2] > gates[j])
            cnt = cnt + jnp.where(beats, 1.0, 0.0) * (j2 < qi).astype(F32)
        sel_scr[j:j + 1, :] = jnp.where(cnt < float(MOBA_TOPK), 0.0, -NEG_INF)

    sub = lax.broadcasted_iota(jnp.int32, (blk, blk), 0)
    lane = lax.broadcasted_iota(jnp.int32, (blk, blk), 1)
    rel = (lane - sub).astype(F32)
    alibi = slope * rel

    def scores(j, penalty, row_const, m):
        start = pl.multiple_of(j * blk, blk)
        kj = k_ref[:, pl.ds(start, blk)].astype(BF16)
        s = lax.dot_general(kj, qb, _TN, preferred_element_type=F32) - penalty - row_const
        s_scr[pl.ds(start, blk), :] = s
        return jnp.maximum(m, jnp.max(s, axis=0, keepdims=True))

    def past(j, m):
        row_const = slope * ((qi - j) * blk).astype(F32) + sel_scr[pl.ds(j, 1), :]
        return scores(j, alibi, row_const, m)

    m = scores(qi, jnp.where(rel >= 0.0, alibi, -NEG_INF), 0.0, jnp.full((1, blk), NEG_INF, F32))
    m = _loop_tiles(0, qi, past, m)

    def accumulate(j, carry):
        l, acc = carry
        start = pl.multiple_of(j * blk, blk)
        p = jnp.exp(s_scr[pl.ds(start, blk), :] - m)
        vj = v_ref[:, pl.ds(start, blk)].astype(BF16)
        return (l + jnp.sum(p, axis=0, keepdims=True),
                acc + jnp.dot(vj, p.astype(BF16), preferred_element_type=F32))

    l, acc = _loop_tiles(0, qi + 1, accumulate, (jnp.zeros((1, blk), F32), jnp.zeros((HEAD_DIM, blk), F32)))
    o_ref[...] = acc / l


def _moba_prompt(yt, slopes, q_row0, k_row0, v_row0):
    bsz, _, seq = yt.shape
    assert seq % MOBA_BLOCK == 0
    n_blk = seq // MOBA_BLOCK
    hd = HEAD_DIM

    def kvspec(row0):
        return pl.BlockSpec((None, hd, seq), lambda b, h, i: (b, row0 // hd + h, 0))

    return pl.pallas_call(
        functools.partial(_moba_p_kernel, n_blk=n_blk),
        grid=(bsz, MOBA_HEADS, n_blk),
        in_specs=[
            pl.BlockSpec(memory_space=pltpu.SMEM),
            pl.BlockSpec((None, hd, MOBA_BLOCK), lambda b, h, i: (b, q_row0 // hd + h, i)),
            kvspec(k_row0),
            kvspec(v_row0),
        ],
        out_specs=pl.BlockSpec((None, hd, MOBA_BLOCK), lambda b, h, i: (b, h, i)),
        out_shape=jax.ShapeDtypeStruct((bsz, MOBA_HEADS * hd, seq), F32),
        scratch_shapes=[pltpu.VMEM((max(n_blk, SUBLANES), MOBA_BLOCK), F32), pltpu.VMEM((seq, MOBA_BLOCK), F32),
                        pltpu.VMEM((HEAD_DIM, LANES), F32)],
        compiler_params=_cparams("parallel", "parallel", "arbitrary"),
        name="moba_prompt",
    )(slopes, yt, yt, yt)


def _log_sigmoid(x):
    return jnp.minimum(x, 0.0) - jnp.log1p(jnp.exp(-jnp.abs(x)))


def _gla_p_kernel(q_ref, k_ref, v_ref, ga_ref, rg_ref, gw_ref, gb_ref, on_ref, o_ref, s_ref, st_scr):
    ti = pl.program_id(1)
    ch = GLA_CHUNK
    n_ch = LANES // ch

    @pl.when(ti == 0)
    def _():
        st_scr[...] = jnp.zeros_like(st_scr)

    ga = ga_ref[0:GLA_GATE_RANK, :].astype(BF16)
    lane_k = lax.broadcasted_iota(jnp.int32, (GLA_DK, LANES), 1)
    pos = lane_k % ch
    cid_v = lax.broadcasted_iota(jnp.int32, (GLA_DV, LANES), 1) // ch
    cid_k = lane_k // ch

    def head(hh):
        q = q_ref[hh * GLA_DK:(hh + 1) * GLA_DK, :]
        k = k_ref[hh * GLA_DK:(hh + 1) * GLA_DK, :]
        v = v_ref[hh * GLA_DV:(hh + 1) * GLA_DV, :]
        x = jnp.dot(gw_ref[hh].astype(BF16), ga, preferred_element_type=F32) + gb_ref[hh]
        g = _log_sigmoid(x) / GLA_TAU

        b = g
        c = g
        for sh in (1, 2, 4, 8):
            b = b + jnp.where(pos >= sh, pltpu.roll(b, sh, 1), 0.0)
            c = c + jnp.where(pos < ch - sh, pltpu.roll(c, LANES - sh, 1), 0.0)
        tot = b + c - g

        qd = q * jnp.exp(b)
        kd = k * jnp.exp(tot - b)

        o = jnp.zeros((GLA_DV, LANES), F32)
        for dl in range(ch):
            if dl == 0:
                kk, bb, vv = k, b, v
            else:
                kk, bb, vv = pltpu.roll(k, dl, 1), pltpu.roll(b, dl, 1), pltpu.roll(v, dl, 1)
            a = jnp.where(pos >= dl, q * kk * jnp.exp(b - bb), 0.0)
            o = o + jnp.sum(a, axis=0, keepdims=True) * vv

        vm = jnp.concatenate([jnp.where(cid_v == cc, v, 0.0) for cc in range(n_ch)], axis=0).astype(BF16)
        ut = lax.dot_general(vm, kd.astype(BF16), _NT, preferred_element_type=F32)
        dt = jnp.exp(tot).T
        st = st_scr[hh]
        pieces = []
        for cc in range(n_ch):
            pieces.append(st.astype(BF16))
            st = st * dt[cc * ch:cc * ch + 1, :] + ut[cc * GLA_DV:(cc + 1) * GLA_DV, :]
        st_scr[hh] = st
        sstack = jnp.concatenate(pieces, axis=1)
        qm = jnp.concatenate([jnp.where(cid_k == cc, qd, 0.0) for cc in range(n_ch)], axis=0).astype(BF16)
        o = o + jnp.dot(sstack, qm, preferred_element_type=F32)

        ms = jnp.mean(o * o, axis=0, keepdims=True)
        rg = rg_ref[hh * GLA_DV:(hh + 1) * GLA_DV, :]
        o_ref[hh * GLA_DV:(hh + 1) * GLA_DV, :] = (o * lax.rsqrt(ms + RMS_EPS) * on_ref[hh]) * _silu(rg)

        @pl.when(ti == pl.num_programs(1) - 1)
        def _():
            s_ref[hh] = st.T

    for hh in range(GLA_HEADS):
        head(hh)


def _gla_prompt(yt, rows, gw_t, gb, on):
    bsz, _, seq = yt.shape
    nt = seq // LANES
    q0, k0, v0, rg0, ga0 = rows
    nk, nv = GLA_HEADS * GLA_DK, GLA_HEADS * GLA_DV
    assert q0 % nk == 0 and k0 % nk == 0 and v0 % nv == 0 and rg0 % nv == 0 and ga0 % LANES == 0

    def at(row0, size):
        return pl.BlockSpec((None, size, LANES), lambda b, t: (b, row0 // size, t))

    full = lambda a: pl.BlockSpec(a.shape, lambda b, t: (0, 0, 0))
    return pl.pallas_call(
        _gla_p_kernel,
        grid=(bsz, nt),
        in_specs=[at(q0, nk), at(k0, nk), at(v0, nv), at(ga0, LANES), at(rg0, nv), full(gw_t), full(gb), full(on)],
        out_specs=[
            pl.BlockSpec((None, nv, LANES), lambda b, t: (b, 0, t)),
            pl.BlockSpec((None, GLA_HEADS, GLA_DK, GLA_DV), lambda b, t: (b, 0, 0, 0)),
        ],
        out_shape=[
            jax.ShapeDtypeStruct((bsz, nv, seq), F32),
            jax.ShapeDtypeStruct((bsz, GLA_HEADS, GLA_DK, GLA_DV), F32),
        ],
        scratch_shapes=[pltpu.VMEM((GLA_HEADS, GLA_DV, GLA_DK), F32)],
        compiler_params=_cparams("parallel", "arbitrary"),
        name="gla_prompt",
    )(yt, yt, yt, yt, yt, gw_t, gb, on)


def _gla_s_kernel(q_ref, k_ref, v_ref, rg_ref, ga_ref, gw_ref, gb_ref, on_ref, s0_ref, o_ref, s_ref):
    x = jnp.sum(gw_ref[...] * ga_ref[...], axis=-1, keepdims=True) + gb_ref[...]
    a = jnp.exp(_log_sigmoid(x) / GLA_TAU)
    s = a * s0_ref[...] + k_ref[...] * v_ref[...]
    s_ref[...] = s
    o = jnp.sum(q_ref[...] * s, axis=1, keepdims=True)
    ms = jnp.mean(o * o, axis=-1, keepdims=True)
    o_ref[...] = (o * lax.rsqrt(ms + RMS_EPS) * on_ref[...]) * _silu(rg_ref[...])


def _gla_sample(q, k, v, rg, ga, gw_t, gb, on, s0):
    bsz = q.shape[0]
    per_b = lambda a: pl.BlockSpec((None,) + a.shape[1:], lambda b: (b, 0, 0, 0))
    full = lambda a: pl.BlockSpec(a.shape, lambda b: (0,) * a.ndim)
    return pl.pallas_call(
        _gla_s_kernel,
        grid=(bsz,),
        in_specs=[per_b(q), per_b(k), per_b(v), per_b(rg), per_b(ga), full(gw_t), full(gb), full(on), per_b(s0)],
        out_specs=[per_b(v), per_b(s0)],
        out_shape=[jax.ShapeDtypeStruct(v.shape, F32), jax.ShapeDtypeStruct(s0.shape, F32)],
        compiler_params=_cparams("parallel"),
        name="gla_sample",
    )(q, k, v, rg, ga, gw_t, gb, on, s0)


MOBA_PAGES_PER_STEP = 16


def _moba_gate_kernel(pt_ref, q_ref, *refs, n_blk):
    pages, (gate_ref, top_ref) = refs[:MOBA_PAGES_PER_STEP], refs[MOBA_PAGES_PER_STEP:]
    s = pl.program_id(1)
    per_step = MOBA_PAGES_PER_STEP // 2
    lane = lax.broadcasted_iota(jnp.int32, (MOBA_HEADS, LANES), 1)

    @pl.when(s == 0)
    def _():
        gate_ref[...] = jnp.full_like(gate_ref, NEG_INF)

    q = _r16(q_ref[...])
    gate = gate_ref[...]
    for u in range(per_step):
        ksum = pages[2 * u][...] + pages[2 * u + 1][...]
        kmean = jnp.sum(ksum, axis=-1, keepdims=True) / float(MOBA_BLOCK)
        gv = jnp.sum(q * _r16(kmean), axis=1)
        gate = jnp.where(lane == s * per_step + u, gv, gate)
    gate_ref[...] = gate

    @pl.when(s == pl.num_programs(1) - 1)
    def _():
        g = gate
        top = jnp.zeros((MOBA_HEADS, LANES), jnp.int32)
        for r in range(MOBA_TOPK):
            m = jnp.max(g, axis=-1, keepdims=True)
            idx = jnp.min(jnp.where(g == m, lane, LANES), axis=-1, keepdims=True)
            top = jnp.where(lane == r, idx, top)
            g = jnp.where(lane == idx, -jnp.inf, g)
        top_ref[...] = top


def _moba_gates(page_table, q, kpool_t):
    bsz, n_pages = page_table.shape
    n_blk = n_pages // 2
    assert MOBA_TOPK <= n_blk <= LANES and n_pages % MOBA_PAGES_PER_STEP == 0
    pps = MOBA_PAGES_PER_STEP

    def page_spec(u):
        return pl.BlockSpec((None, MOBA_HEADS, HEAD_DIM, LANES), lambda b, s, pt: (pt[b, s * pps + u], 0, 0, 0))

    out_spec = pl.BlockSpec((None, MOBA_HEADS, LANES), lambda b, s, pt: (b, 0, 0))
    return pl.pallas_call(
        functools.partial(_moba_gate_kernel, n_blk=n_blk),
        grid_spec=pltpu.PrefetchScalarGridSpec(
            num_scalar_prefetch=1,
            grid=(bsz, n_pages // pps),
            in_specs=[pl.BlockSpec((None, MOBA_HEADS, HEAD_DIM, 1), lambda b, s, pt: (b, 0, 0, 0))]
            + [page_spec(u) for u in range(pps)],
            out_specs=[out_spec, out_spec],
        ),
        out_shape=[jax.ShapeDtypeStruct((bsz, MOBA_HEADS, LANES), F32),
                   jax.ShapeDtypeStruct((bsz, MOBA_HEADS, LANES), jnp.int32)],
        compiler_params=_cparams("parallel", "arbitrary"),
        name="moba_gates",
    )(page_table, q, *([kpool_t] * pps))


def _moba_s_kernel(pt_ref, top_ref, slopes_ref, q_ref, kn_ref, vn_ref, *refs, past):
    n_pg = 2 * MOBA_TOPK
    k_refs, v_refs, o_ref = refs[:n_pg], refs[n_pg:2 * n_pg], refs[2 * n_pg]
    b = pl.program_id(0)
    h = pl.program_id(1)
    slope = slopes_ref[h]
    q = _r16(q_ref[...])
    lane = lax.broadcasted_iota(jnp.int32, (1, LANES), 1)
    scores = []
    for r in range(MOBA_TOPK):
        for half in range(2):
            kpos = top_ref[b, h * MOBA_TOPK + r] * MOBA_BLOCK + half * LANES + lane
            s = jnp.sum(_r16(k_refs[2 * r + half][...]) * q, axis=0, keepdims=True)
            scores.append(s - slope * (past - kpos).astype(F32))
    s_new = jnp.sum(_r16(kn_ref[...]) * q, axis=0, keepdims=True)
    m = s_new
    for s in scores:
        m = jnp.maximum(m, jnp.max(s, axis=-1, keepdims=True))
    p_new = jnp.exp(s_new - m)
    probs = [jnp.exp(s - m) for s in scores]
    l = p_new
    for p in probs:
        l = l + jnp.sum(p, axis=-1, keepdims=True)
    o = _r16(p_new / l) * _r16(vn_ref[...])
    for p, v_ref in zip(probs, v_refs):
        o = o + jnp.sum(_r16(v_ref[...]) * _r16(p / l), axis=-1, keepdims=True)
    o_ref[...] = o


def _moba_sample(page_table, top, slopes, q, k_new, v_new, kpool_t, vpool_t):
    bsz, n_pages = page_table.shape
    past = n_pages * LANES

    def col_spec():
        return pl.BlockSpec((None, None, HEAD_DIM, 1), lambda b, h, pt, tp: (b, h, 0, 0))

    def page_spec(r, half):
        return pl.BlockSpec((None, None, HEAD_DIM, LANES),
                            lambda b, h, pt, tp: (pt[b, 2 * tp[b, h * MOBA_TOPK + r] + half], h, 0, 0))

    pages = [page_spec(r, half) for r in range(MOBA_TOPK) for half in range(2)]
    return pl.pallas_call(
        functools.partial(_moba_s_kernel, past=past),
        grid_spec=pltpu.PrefetchScalarGridSpec(
            num_scalar_prefetch=2,
            grid=(bsz, MOBA_HEADS),
            in_specs=[pl.BlockSpec(memory_space=pltpu.SMEM), col_spec(), col_spec(), col_spec()] + pages + pages,
            out_specs=col_spec(),
        ),
        out_shape=jax.ShapeDtypeStruct((bsz, MOBA_HEADS, HEAD_DIM, 1), F32),
        compiler_params=_cparams("parallel", "parallel"),
        name="moba_sample",
    )(page_table, top, slopes, q, k_new, v_new, *([kpool_t] * len(pages)), *([vpool_t] * len(pages)))


def _cmp_stage1_kernel(*refs, n_in):
    x_refs, w_ref, o_ref, x_scr = refs[:n_in], refs[n_in], refs[n_in + 1], refs[n_in + 2]
    st = NSA_CMP_STRIDE
    row = 0
    for r in x_refs:
        w = r.shape[-1]
        x_scr[row:row + w, :] = r[...].T
        row += w
    n_chunk = row // st
    xcat = jnp.concatenate([x_scr[pl.ds(s, n_chunk, stride=st), :] for s in range(st)], axis=1)
    o_ref[...] = jnp.dot(xcat.astype(BF16), w_ref[...], preferred_element_type=F32)


def _cmp_w1cat(w1):
    ratio = NSA_CMP_LEN // NSA_CMP_STRIDE
    w = w1.reshape(ratio, NSA_CMP_STRIDE * HEAD_DIM, NSA_CMP_HIDDEN)
    return jnp.concatenate([w[r] for r in range(ratio)], axis=1).astype(BF16)


def _cmp_stage1_dense(xt, row0, w1cat):
    bsz, _, seq = xt.shape
    n_chunk = seq // NSA_CMP_STRIDE
    return pl.pallas_call(
        functools.partial(_cmp_stage1_kernel, n_in=1),
        grid=(bsz, NSA_KV_HEADS),
        in_specs=[pl.BlockSpec((None, HEAD_DIM, seq), lambda b, g: (b, row0 // HEAD_DIM + g, 0)),
                  pl.BlockSpec(w1cat.shape, lambda b, g: (0, 0))],
        out_specs=pl.BlockSpec((None, None, n_chunk, LANES), lambda b, g: (b, g, 0, 0)),
        out_shape=jax.ShapeDtypeStruct((bsz, NSA_KV_HEADS, n_chunk, LANES), F32),
        scratch_shapes=[pltpu.VMEM((seq, HEAD_DIM), F32)],
        compiler_params=_cparams("parallel", "parallel"),
        name="cmp_stage1_dense",
    )(xt, w1cat)


CMP_PAGES_PER_STEP = 16


def _cmp_stage1_paged_kernel(pt_ref, *refs):
    pps, ng, st = CMP_PAGES_PER_STEP, NSA_KV_HEADS, NSA_CMP_STRIDE
    x_refs, w_ref, o_ref, x_scr = refs[:pps], refs[pps], refs[pps + 1], refs[pps + 2]
    for gp in range(ng // 2):
        for u, r in enumerate(x_refs):
            row = (gp * pps + u) * LANES
            x_scr[row:row + LANES, :] = jnp.concatenate([r[2 * gp], r[2 * gp + 1]], axis=0).T
    n_rows = (ng // 2) * pps * LANES // st
    xcat = jnp.concatenate([x_scr[pl.ds(s, n_rows, stride=st), :] for s in range(st)], axis=1)
    y = jnp.dot(xcat.astype(BF16), w_ref[...], preferred_element_type=F32)
    cps = pps * LANES // st
    for gp in range(ng // 2):
        for gl in range(2):
            o_ref[2 * gp + gl] = y[gp * cps:(gp + 1) * cps, gl * LANES:(gl + 1) * LANES]


def _cmp_stage1_paged(page_table, pool_t, w1cat):
    bsz, n_pages = page_table.shape
    pps = CMP_PAGES_PER_STEP
    assert n_pages % pps == 0 and NSA_KV_HEADS % 2 == 0
    cps = pps * LANES // NSA_CMP_STRIDE
    n_chunk = n_pages * LANES // NSA_CMP_STRIDE
    w3 = w1cat.reshape(NSA_CMP_STRIDE, 1, HEAD_DIM, 1, LANES)
    eye = jnp.eye(2, dtype=w1cat.dtype).reshape(1, 2, 1, 2, 1)
    w1cat = (w3 * eye).reshape(NSA_CMP_STRIDE * 2 * HEAD_DIM, 2 * LANES)

    def page_spec(u):
        return pl.BlockSpec((None, NSA_KV_HEADS, HEAD_DIM, LANES), lambda b, s, pt: (pt[b, s * pps + u], 0, 0, 0))

    return pl.pallas_call(
        _cmp_stage1_paged_kernel,
        grid_spec=pltpu.PrefetchScalarGridSpec(
            num_scalar_prefetch=1,
            grid=(bsz, n_pages // pps),
            in_specs=[page_spec(u) for u in range(pps)] + [pl.BlockSpec(w1cat.shape, lambda b, s, pt: (0, 0))],
            out_specs=pl.BlockSpec((None, NSA_KV_HEADS, cps, LANES), lambda b, s, pt: (b, 0, s, 0)),
            scratch_shapes=[pltpu.VMEM((NSA_KV_HEADS // 2 * pps * LANES, LANES), F32)],
        ),
        out_shape=jax.ShapeDtypeStruct((bsz, NSA_KV_HEADS, n_chunk, LANES), F32),
        compiler_params=_cparams("parallel", "arbitrary"),
        name="cmp_stage1_paged",
    )(page_table, *([pool_t] * pps), w1cat)


def _cmp_stage2_kernel(p_ref, pe_ref, w1_ref, b1_ref, w2_ref, b2_ref, gain_ref, o_ref, *, normalize):
    p = p_ref[...]
    n = p.shape[0]
    const = jnp.dot(pe_ref[...].astype(BF16), w1_ref[...].astype(BF16), preferred_element_type=F32) + b1_ref[...]
    shifted = pltpu.roll(pltpu.roll(p, n - 1, 0), NSA_CMP_HIDDEN, 1)
    hid = (p + shifted)[:, :NSA_CMP_HIDDEN] + const
    out = jnp.dot(_silu(hid).astype(BF16), w2_ref[...], preferred_element_type=F32) + b2_ref[...]
    if normalize:
        ms = jnp.mean(out * out, axis=-1, keepdims=True)
        out = out * lax.rsqrt(ms + RMS_EPS) * gain_ref[...]
    o_ref[...] = out.T


def _cmp_stage2(parts, pe, w1, b1, w2, b2, gain):
    bsz, ng, n_chunk, _ = parts.shape
    normalize = gain is not None
    gain = jnp.ones((HEAD_DIM,), F32) if gain is None else gain
    full = lambda a: pl.BlockSpec(a.shape, lambda b, g: (0,) * a.ndim)
    args = (pe.reshape(1, -1), w1, b1.reshape(1, -1), w2.astype(BF16), b2.reshape(1, -1), gain.reshape(1, -1))
    return pl.pallas_call(
        functools.partial(_cmp_stage2_kernel, normalize=normalize),
        grid=(bsz, ng),
        in_specs=[pl.BlockSpec((None, None, n_chunk, LANES), lambda b, g: (b, g, 0, 0))] + [full(a) for a in args],
        out_specs=pl.BlockSpec((None, None, HEAD_DIM, n_chunk), lambda b, g: (b, g, 0, 0)),
        out_shape=jax.ShapeDtypeStruct((bsz, ng, HEAD_DIM, n_chunk), F32),
        compiler_params=_cparams("parallel", "parallel"),
        name="cmp_stage2",
    )(parts, *args)


NSA_TQ = 128


def _nsa_p_kernel(slopes_ref, q_ref, kc_ref, vc_ref, ks_ref, vs_ref, kw_ref, vw_ref, gt_ref, gb_ref, ov_ref,
                  o_ref, sel_scr, s_slc, s_win, *, n_cmp, n_blk):
    g = pl.program_id(1)
    qi = pl.program_id(2)
    tq, grp, hd = NSA_TQ, NSA_GROUP, HEAD_DIM
    wq = grp * tq
    q4 = jnp.concatenate([q_ref[r * hd:(r + 1) * hd, :] for r in range(grp)], axis=1)
    qb = q4.astype(BF16)
    lane = lax.broadcasted_iota(jnp.int32, (1, wq), 1)
    qpos = qi * tq + lane % tq
    slope = jnp.zeros((1, wq), F32)
    for r in range(grp):
        slope = jnp.where(lane // tq == r, slopes_ref[g * grp + r], slope)

    def softmax_block(kt, vt, kpos, mask, carry):
        m, l, acc = carry
        s = lax.dot_general(kt, qb, _TN, preferred_element_type=F32)
        s = jnp.where(mask, s - slope * (qpos - kpos).astype(F32), NEG_INF)
        m_new = jnp.maximum(m, jnp.max(s, axis=0, keepdims=True))
        p = jnp.where(mask, jnp.exp(s - m_new), 0.0)
        alpha = jnp.exp(m - m_new)
        l = alpha * l + jnp.sum(p, axis=0, keepdims=True)
        acc = alpha * acc + jnp.dot(vt, p.astype(BF16), preferred_element_type=F32)
        return (m_new, l, acc), p

    def finish(carry):
        _, l, acc = carry
        return acc / jnp.where(l > 0.0, l, 1.0)

    init = (jnp.full((1, wq), NEG_INF, F32), jnp.zeros((1, wq), F32), jnp.zeros((hd, wq), F32))

    nc = kc_ref.shape[-1]
    ci = lax.broadcasted_iota(jnp.int32, (nc, 1), 0)
    c_end = ci * NSA_CMP_STRIDE + (NSA_CMP_LEN - 1)
    mask_c = (c_end <= qpos) & (ci < n_cmp)
    carry_c, p_c = softmax_block(kc_ref[...].astype(BF16), vc_ref[...].astype(BF16), c_end, mask_c, init)
    o_c = finish(carry_c)
    prob_c = p_c / jnp.where(carry_c[1] > 0.0, carry_c[1], 1.0)

    imp4 = jnp.dot(ov_ref[...].astype(BF16), prob_c.astype(BF16), preferred_element_type=F32)
    imp = imp4[:, 0:tq]
    for r in range(1, grp):
        imp = imp + imp4[:, r * tq:(r + 1) * tq]
    bi = lax.broadcasted_iota(jnp.int32, (n_blk, tq), 0)
    cur = (qi * tq + lax.broadcasted_iota(jnp.int32, (n_blk, tq), 1)) // NSA_SLC_BLOCK
    allowed = bi <= cur
    forced = (bi < NSA_INIT_BLOCKS) | (cur - bi < NSA_LOCAL_BLOCKS)
    val = jnp.where(allowed & forced, FORCE_SCORE, jnp.where(allowed, imp, NEG_INF))
    rank = jnp.zeros((n_blk, tq), F32)
    for j in range(n_blk):
        vj = val[j:j + 1, :]
        beats = (vj > val) | ((vj == val) & (j < bi))
        rank = rank + jnp.where(beats, 1.0, 0.0)
    big = -NEG_INF
    pen = jnp.where((rank < float(NSA_SLC_TOPN)) & allowed, 0.0, big)
    sel_scr[0:n_blk, :] = jnp.concatenate([pen] * grp, axis=1)

    half = NSA_SLC_BLOCK
    ksub = lax.broadcasted_iota(jnp.int32, (tq, 1), 0)
    rel = (lane % tq - ksub).astype(F32)
    alibi = slope * rel
    pen_diag = jnp.where(rel >= 0.0, alibi, big)
    pen_low = jnp.where(rel <= 0.0, alibi, big)

    def scores(k_ref_, scr, jj, row, penalty, parts, m):
        start = pl.multiple_of(jj * tq, tq)
        kt = k_ref_[:, pl.ds(start, tq)].astype(BF16)
        s = lax.dot_general(kt, qb, _TN, preferred_element_type=F32) - penalty
        for r0, r1, c in parts:
            sp = s[r0:r1] - c
            scr[pl.ds(pl.multiple_of(row + r0, half), r1 - r0), :] = sp
            m = jnp.maximum(m, jnp.max(sp, axis=0, keepdims=True))
        return m

    def accumulate(v_ref_, scr, jj, row, m, carry):
        l, acc = carry
        start = pl.multiple_of(jj * tq, tq)
        p = jnp.exp(scr[pl.ds(pl.multiple_of(row, tq), tq), :] - m)
        vt = v_ref_[:, pl.ds(start, tq)].astype(BF16)
        return (l + jnp.sum(p, axis=0, keepdims=True),
                acc + jnp.dot(vt, p.astype(BF16), preferred_element_type=F32))

    def slc_parts(jj, row_const):
        return [(0, half, row_const + sel_scr[pl.ds(2 * jj, 1), :]),
                (half, tq, row_const + sel_scr[pl.ds(2 * jj + 1, 1), :])]

    def tile_const(jj):
        return slope * ((qi - jj) * tq).astype(F32)

    m0 = jnp.full((1, wq), NEG_INF, F32)
    la0 = (jnp.zeros((1, wq), F32), jnp.zeros((hd, wq), F32))

    m = scores(ks_ref, s_slc, qi, qi * tq, pen_diag, slc_parts(qi, 0.0), m0)
    m = _loop_tiles(
        0, qi, lambda jj, mm: scores(ks_ref, s_slc, jj, jj * tq, alibi, slc_parts(jj, tile_const(jj)), mm), m)
    l, acc = _loop_tiles(0, qi + 1, lambda jj, c: accumulate(vs_ref, s_slc, jj, jj * tq, m, c), la0)
    o_s = acc / l

    n_wt = NSA_WINDOW // tq
    wrow = lambda jj: (jj - (qi - n_wt)) * tq
    first = jnp.maximum(qi - (n_wt - 1), 0)
    m = scores(kw_ref, s_win, qi, wrow(qi), pen_diag, [(0, tq, 0.0)], m0)
    m = _loop_tiles(
        first, qi, lambda jj, mm: scores(kw_ref, s_win, jj, wrow(jj), alibi, [(0, tq, tile_const(jj))], mm), m)
    has_low = jnp.where(qi >= n_wt, 1, 0)
    m = lax.fori_loop(
        0, has_low,
        lambda _, mm: scores(kw_ref, s_win, qi - n_wt, wrow(qi - n_wt), pen_low,
                             [(0, tq, tile_const(qi - n_wt))], mm), m)
    l, acc = _loop_tiles(first - has_low, qi + 1, lambda jj, c: accumulate(vw_ref, s_win, jj, wrow(jj), m, c), la0)
    o_w = acc / l

    outs = []
    for r in range(grp):
        acc = jnp.zeros((hd, tq), F32)
        for br, o_b in enumerate((o_c, o_s, o_w)):
            row = (g * grp + r) * 3 + br
            gt = gt_ref[pl.ds(row, 1), :] + gb_ref[pl.ds(row, 1), :]
            acc = acc + (1.0 / (1.0 + jnp.exp(-gt))) * o_b[:, r * tq:(r + 1) * tq]
        outs.append(acc)
    o_ref[...] = jnp.concatenate(outs, axis=0)


def _nsa_overlap_t(n_cmp_pad, n_blk):
    c_start = np.arange(n_cmp_pad) * NSA_CMP_STRIDE
    c_end = c_start + NSA_CMP_LEN - 1
    b_start = np.arange(n_blk) * NSA_SLC_BLOCK
    ov = (c_start[None, :] < b_start[:, None] + NSA_SLC_BLOCK) & (c_end[None, :] >= b_start[:, None])
    return jnp.asarray(ov.astype(np.float32))


def _nsa_prompt(yt, kcmp_t, vcmp_t, slopes, rows, gate_b_col):
    bsz, _, seq = yt.shape
    q0, ks0, vs0, kw0, vw0, g0 = rows
    n_chunk = kcmp_t.shape[-1]
    n_cmp = max((seq - NSA_CMP_LEN) // NSA_CMP_STRIDE + 1, 1)
    n_blk = max(-(-seq // NSA_SLC_BLOCK), NSA_SLC_TOPN)
    assert seq % NSA_TQ == 0 and n_blk * NSA_SLC_BLOCK == seq and n_blk % SUBLANES == 0
    assert NSA_TQ == 2 * NSA_SLC_BLOCK and NSA_WINDOW % NSA_TQ == 0 and NSA_LOCAL_BLOCKS >= 1
    hd, wq = HEAD_DIM, NSA_GROUP * NSA_TQ
    ov = _nsa_overlap_t(n_chunk, n_blk)

    def kv(row0):
        return pl.BlockSpec((None, hd, seq), lambda b, g, i: (b, row0 // hd + g, 0))

    cmp_spec = pl.BlockSpec((None, None, hd, n_chunk), lambda b, g, i: (b, g, 0, 0))
    return pl.pallas_call(
        functools.partial(_nsa_p_kernel, n_cmp=n_cmp, n_blk=n_blk),
        grid=(bsz, NSA_KV_HEADS, seq // NSA_TQ),
        in_specs=[
            pl.BlockSpec(memory_space=pltpu.SMEM),
            pl.BlockSpec((None, NSA_GROUP * hd, NSA_TQ), lambda b, g, i: (b, q0 // (NSA_GROUP * hd) + g, i)),
            cmp_spec, cmp_spec, kv(ks0), kv(vs0), kv(kw0), kv(vw0),
            pl.BlockSpec((None, LANES, NSA_TQ), lambda b, g, i: (b, g0 // LANES, i)),
            pl.BlockSpec((LANES, 1), lambda b, g, i: (0, 0)),
            pl.BlockSpec(ov.shape, lambda b, g, i: (0, 0)),
        ],
        out_specs=pl.BlockSpec((None, NSA_GROUP * hd, NSA_TQ), lambda b, g, i: (b, g, i)),
        out_shape=jax.ShapeDtypeStruct((bsz, NSA_HEADS * hd, seq), F32),
        scratch_shapes=[pltpu.VMEM((n_blk, wq), F32), pltpu.VMEM((seq, wq), F32),
                        pltpu.VMEM((NSA_WINDOW + NSA_TQ, wq), F32)],
        compiler_params=_cparams("parallel", "parallel", "arbitrary"),
        name="nsa_prompt",
    )(slopes, yt, kcmp_t, vcmp_t, yt, yt, yt, yt, yt, gate_b_col, ov)


def _nsa_s_cmp_kernel(slopes_ref, q_ref, kc_ref, vc_ref, ov_ref, oc_ref, sel_ref, *, past, n_cmp, n_blk):
    g = pl.program_id(1)
    grp = NSA_GROUP
    nc = kc_ref.shape[-1]
    nb = ov_ref.shape[0]
    ci = lax.broadcasted_iota(jnp.int32, (1, nc), 1)
    c_end = ci * NSA_CMP_STRIDE + (NSA_CMP_LEN - 1)
    mask = (c_end <= past) & (ci < n_cmp)
    dist = (past - c_end).astype(F32)
    kc = _r16(kc_ref[...])
    vc = _r16(vc_ref[...])
    probs = []
    for r in range(grp):
        s = jnp.sum(kc * _r16(q_ref[r]), axis=0, keepdims=True)
        s = jnp.where(mask, s - slopes_ref[g * grp + r] * dist, NEG_INF)
        m = jnp.max(s, axis=-1, keepdims=True)
        p = jnp.where(mask, jnp.exp(s - m), 0.0)
        l = jnp.sum(p, axis=-1, keepdims=True)
        p = _r16(p / jnp.where(l > 0.0, l, 1.0))
        probs.append(p)
        oc_ref[r] = jnp.sum(vc * p, axis=-1, keepdims=True)
    p8 = jnp.concatenate(probs + [jnp.zeros((SUBLANES - grp, nc), F32)], axis=0).astype(BF16)
    imp = jnp.sum(lax.dot_general(p8, ov_ref[...].astype(BF16), _NT, preferred_element_type=F32),
                  axis=0, keepdims=True)

    bi = lax.broadcasted_iota(jnp.int32, (1, nb), 1)
    cur = past // NSA_SLC_BLOCK
    allowed = (bi <= cur) & (bi < n_blk)
    forced = (bi < NSA_INIT_BLOCKS) | (cur - bi < NSA_LOCAL_BLOCKS)
    val = jnp.where(allowed & forced, FORCE_SCORE, jnp.where(allowed, imp, NEG_INF))
    val = jnp.where(bi < n_blk, val, -jnp.inf)
    bj = lax.broadcasted_iota(jnp.int32, (nb, 1), 0)
    val_col = jnp.sum(jnp.where(bj == bi, val, 0.0), axis=-1, keepdims=True)
    beats = (val_col > val) | ((val_col == val) & (bj < bi))
    rank = jnp.sum(jnp.where(beats, 1.0, 0.0), axis=0, keepdims=True)
    lane = lax.broadcasted_iota(jnp.int32, (1, LANES), 1)
    out = jnp.full((1, LANES), -1, jnp.int32)
    for k in range(NSA_SLC_TOPN):
        hit = (rank == float(k)) & allowed
        idx = jnp.sum(jnp.where(hit, bi, 0), axis=-1, keepdims=True)
        cnt = jnp.sum(jnp.where(hit, 1, 0), axis=-1, keepdims=True)
        out = jnp.where(lane == k, jnp.where(cnt > 0, idx, -1), out)
    sel_ref[...] = out


def _nsa_sample_cmp(q, kcmp_t, vcmp_t, slopes, past, n_keys):
    bsz = q.shape[0]
    n_chunk = kcmp_t.shape[-1]
    n_cmp = max((n_keys - NSA_CMP_LEN) // NSA_CMP_STRIDE + 1, 1)
    n_blk = max(-(-n_keys // NSA_SLC_BLOCK), NSA_SLC_TOPN)
    nb_pad = -(-n_blk // LANES) * LANES
    ov = _nsa_overlap_t(n_chunk, nb_pad)
    qspec = pl.BlockSpec((None, None, NSA_GROUP, HEAD_DIM, 1), lambda b, g: (b, g, 0, 0, 0))
    cspec = pl.BlockSpec((None, None, HEAD_DIM, n_chunk), lambda b, g: (b, g, 0, 0))
    return pl.pallas_call(
        functools.partial(_nsa_s_cmp_kernel, past=past, n_cmp=n_cmp, n_blk=n_blk),
        grid=(bsz, NSA_KV_HEADS),
        in_specs=[pl.BlockSpec(memory_space=pltpu.SMEM), qspec, cspec, cspec,
                  pl.BlockSpec(ov.shape, lambda b, g: (0, 0))],
        out_specs=[qspec, pl.BlockSpec((None, None, 1, LANES), lambda b, g: (b, g, 0, 0))],
        out_shape=[jax.ShapeDtypeStruct(q.shape, F32),
                   jax.ShapeDtypeStruct((bsz, NSA_KV_HEADS, 1, LANES), jnp.int32)],
        compiler_params=_cparams("parallel", "parallel"),
        name="nsa_sample_cmp",
    )(slopes, q, kcmp_t, vcmp_t, ov)


def _nsa_s_attn_kernel(pt_ref, sel_ref, slopes_ref, q_ref, oc_ref, gt_ref, gb_ref, ksn_ref, vsn_ref, kwn_ref,
                       vwn_ref, kw_ref, vw_ref, *refs, past):
    n_sel = NSA_SLC_TOPN
    k_refs, v_refs, o_ref = refs[:n_sel], refs[n_sel:2 * n_sel], refs[2 * n_sel]
    b = pl.program_id(0)
    g = pl.program_id(1)
    grp = NSA_GROUP
    pad = jnp.zeros((SUBLANES - grp, HEAD_DIM), F32)
    q8b = jnp.concatenate([q_ref[...], pad], axis=0).astype(BF16)
    q8r = q8b.astype(F32)
    sub = lax.broadcasted_iota(jnp.int32, (SUBLANES, 1), 0)
    slope = jnp.zeros((SUBLANES, 1), F32)
    for r in range(grp):
        slope = jnp.where(sub == r, slopes_ref[g * grp + r], slope)
    lane = lax.broadcasted_iota(jnp.int32, (1, LANES), 1)
    cur = past // NSA_SLC_BLOCK
    blks = [sel_ref[b, g * n_sel + k] for k in range(n_sel)]
    has_new = jnp.zeros((), jnp.int32)
    for blk in blks:
        has_new = jnp.maximum(has_new, (blk == cur).astype(jnp.int32))
    new_ok = jnp.full((1, 1), has_new, jnp.int32) > 0

    s_new = jnp.sum(q8r * _r16(ksn_ref[...]), axis=-1, keepdims=True)
    s_new = jnp.where(new_ok, s_new, NEG_INF)
    scores = []
    m_el = jnp.full((SUBLANES, LANES), NEG_INF, F32)
    for k in range(n_sel):
        blk = blks[k]
        kpos = (blk // 2) * LANES + lane
        valid = (kpos // NSA_SLC_BLOCK == blk) & (blk >= 0) & (kpos < past)
        s = jnp.dot(q8b, k_refs[k][...].astype(BF16), preferred_element_type=F32)
        s = jnp.where(valid, s - slope * (past - kpos).astype(F32), NEG_INF)
        scores.append((s, valid))
        m_el = jnp.maximum(m_el, s)
    m = jnp.maximum(jnp.max(m_el, axis=-1, keepdims=True), s_new)
    p_new = jnp.where(new_ok, jnp.exp(s_new - m), 0.0)
    probs = [jnp.where(valid, jnp.exp(s - m), 0.0) for s, valid in scores]
    l_el = probs[0]
    for p in probs[1:]:
        l_el = l_el + p
    l = jnp.sum(l_el, axis=-1, keepdims=True) + p_new
    inv_l = 1.0 / jnp.where(l > 0.0, l, 1.0)
    o_s = _r16(p_new * inv_l) * _r16(vsn_ref[...])
    for p, v_ref in zip(probs, v_refs):
        o_s = o_s + lax.dot_general((p * inv_l).astype(BF16), v_ref[...].astype(BF16), _NT,
                                    preferred_element_type=F32)

    n_win = kw_ref.shape[-1]
    wpos = past - n_win + lax.broadcasted_iota(jnp.int32, (1, n_win), 1)
    wdist = past - wpos
    wmask = (wpos >= 0) & (wdist <= NSA_WINDOW)
    sw = jnp.dot(q8b, kw_ref[...].astype(BF16), preferred_element_type=F32)
    sw = jnp.where(wmask, sw - slope * wdist.astype(F32), NEG_INF)
    sw_new = jnp.sum(q8r * _r16(kwn_ref[...]), axis=-1, keepdims=True)
    mw = jnp.maximum(jnp.max(sw, axis=-1, keepdims=True), sw_new)
    pw = jnp.where(wmask, jnp.exp(sw - mw), 0.0)
    pw_new = jnp.exp(sw_new - mw)
    inv_lw = 1.0 / (jnp.sum(pw, axis=-1, keepdims=True) + pw_new)
    o_w = (lax.dot_general((pw * inv_lw).astype(BF16), vw_ref[...].astype(BF16), _NT, preferred_element_type=F32)
           + _r16(pw_new * inv_lw) * _r16(vwn_ref[...]))

    gate = 1.0 / (1.0 + jnp.exp(-(gt_ref[...] + gb_ref[...])))
    gcol = []
    for br in range(3):
        col = jnp.zeros((SUBLANES, 1), F32)
        for r in range(grp):
            col = jnp.where(sub == r, gate[r * 3 + br:r * 3 + br + 1, :], col)
        gcol.append(col)
    o_c = jnp.concatenate([oc_ref[...], pad], axis=0)
    o_ref[...] = (gcol[0] * o_c + gcol[1] * o_s + gcol[2] * o_w)[0:grp]


def _nsa_sample_attn(page_table, sel, slopes, q, o_c, gt, gb, new_rows, kwin_t, vwin_t, kpool_t, vpool_t):
    bsz, n_pages = page_table.shape
    past = n_pages * LANES
    n_sel = NSA_SLC_TOPN
    n_win = kwin_t.shape[-1]
    qspec = pl.BlockSpec((None, None, NSA_GROUP, HEAD_DIM), lambda b, g, pt, sl: (b, g, 0, 0))
    col = pl.BlockSpec((None, None, 1, HEAD_DIM), lambda b, g, pt, sl: (b, g, 0, 0))
    win = pl.BlockSpec((None, None, HEAD_DIM, n_win), lambda b, g, pt, sl: (b, g, 0, 0))

    def page_spec(k):
        def index(b, g, pt, sl):
            page = jnp.clip(sl[b, g * n_sel + k] // 2, 0, n_pages - 1)
            return (pt[b, page], g, 0, 0)
        return pl.BlockSpec((None, None, HEAD_DIM, LANES), index)

    pages = [page_spec(k) for k in range(n_sel)]
    return pl.pallas_call(
        functools.partial(_nsa_s_attn_kernel, past=past),
        grid_spec=pltpu.PrefetchScalarGridSpec(
            num_scalar_prefetch=2,
            grid=(bsz, NSA_KV_HEADS),
            in_specs=[pl.BlockSpec(memory_space=pltpu.SMEM), qspec, qspec,
                      pl.BlockSpec((None, None, 3 * NSA_GROUP, 1), lambda b, g, pt, sl: (b, g, 0, 0)),
                      pl.BlockSpec((None, 3 * NSA_GROUP, 1), lambda b, g, pt, sl: (g, 0, 0)),
                      col, col, col, col, win, win] + pages + pages,
            out_specs=qspec,
        ),
        out_shape=jax.ShapeDtypeStruct(q.shape, F32),
        compiler_params=_cparams("parallel", "parallel"),
        name="nsa_sample_attn",
    )(page_table, sel, slopes, q, o_c, gt, gb, *new_rows, kwin_t, vwin_t,
      *([kpool_t] * n_sel), *([vpool_t] * n_sel))


MOE_TOKEN_TILE = 256
MOE_ROW_BLOCK = 512


def _router_kernel(x_ref, g_ref, rt_ref, h_ref, idx_ref, gate_ref):
    x = x_ref[...]
    ms = jnp.mean(x * x, axis=-1, keepdims=True)
    h = x * lax.rsqrt(ms + RMS_EPS) * g_ref[...]
    h_ref[...] = h
    logits = lax.dot_general(rt_ref[...].astype(BF16), h.astype(BF16), _NT, preferred_element_type=F32)
    e_id = lax.broadcasted_iota(jnp.int32, logits.shape, 0)
    v1 = jnp.max(logits, axis=0, keepdims=True)
    i1 = jnp.min(jnp.where(logits == v1, e_id, N_EXPERTS), axis=0, keepdims=True)
    rest = jnp.where(e_id == i1, -jnp.inf, logits)
    v2 = jnp.max(rest, axis=0, keepdims=True)
    i2 = jnp.min(jnp.where(rest == v2, e_id, N_EXPERTS), axis=0, keepdims=True)
    e2 = jnp.exp(v2 - v1)
    idx_ref[...] = jnp.concatenate([i1, i2], axis=0)
    gate_ref[...] = jnp.concatenate([1.0 / (1.0 + e2), e2 / (1.0 + e2)], axis=0)


def _router(x, g, router_t):
    n, d = x.shape
    tm = MOE_TOKEN_TILE
    return pl.pallas_call(
        _router_kernel,
        grid=(n // tm,),
        in_specs=[pl.BlockSpec((tm, d), lambda i: (i, 0)), pl.BlockSpec((1, d), lambda i: (0, 0)),
                  pl.BlockSpec(router_t.shape, lambda i: (0, 0))],
        out_specs=[pl.BlockSpec((tm, d), lambda i: (i, 0)), pl.BlockSpec((TOP_K, tm), lambda i: (0, i)),
                   pl.BlockSpec((TOP_K, tm), lambda i: (0, i))],
        out_shape=[jax.ShapeDtypeStruct((n, d), F32), jax.ShapeDtypeStruct((TOP_K, n), jnp.int32),
                   jax.ShapeDtypeStruct((TOP_K, n), F32)],
        compiler_params=_cparams("parallel"),
        name="moe_router",
    )(x, g, router_t)


def _gather_rows_kernel(idx_ref, src_ref, o_ref, sem):
    rows = o_ref.shape[0]

    def row_copy(r):
        return pltpu.make_async_copy(src_ref.at[pl.ds(idx_ref[0, r], 1)], o_ref.at[pl.ds(r, 1)], sem)

    def start(i, c):
        row_copy(2 * i).start(priority=0)
        row_copy(2 * i + 1).start(priority=1)
        return c

    def wait(r, c):
        row_copy(r).wait()
        return c

    lax.fori_loop(0, rows // 2, start, 0, unroll=4)
    lax.fori_loop(0, rows, wait, 0, unroll=8)


def _gather_rows(src, idx, block):
    m = idx.shape[0]
    d = src.shape[1]
    return pl.pallas_call(
        _gather_rows_kernel,
        grid=(m // block,),
        in_specs=[pl.BlockSpec((None, 1, block), lambda i: (i, 0, 0), memory_space=pltpu.SMEM),
                  pl.BlockSpec(memory_space=pl.ANY)],
        out_specs=pl.BlockSpec((block, d), lambda i: (i, 0)),
        out_shape=jax.ShapeDtypeStruct((m, d), src.dtype),
        scratch_shapes=[pltpu.SemaphoreType.DMA(())],
        compiler_params=_cparams("arbitrary"),
        name="gather_rows",
    )(idx.reshape(m // block, 1, block), src)


def _experts_kernel(be_ref, act_ref, x_ref, wg_ref, wu_ref, wd_ref, o_ref, acc_scr):
    i = pl.program_id(0)
    j = pl.program_id(1)

    @pl.when(act_ref[i] > 0)
    def _():
        @pl.when(j == 0)
        def _():
            acc_scr[...] = jnp.zeros_like(acc_scr)

        xb = x_ref[...].astype(BF16)
        gate = jnp.dot(xb, wg_ref[...], preferred_element_type=F32)
        up = jnp.dot(xb, wu_ref[...], preferred_element_type=F32)
        acc_scr[...] += jnp.dot((_silu(gate) * up).astype(BF16), wd_ref[...], preferred_element_type=F32)

    @pl.when(j == pl.num_programs(1) - 1)
    def _():
        o_ref[...] = jnp.where(act_ref[i] > 0, acc_scr[...], 0.0)


def _experts(block_expert, block_active, xg, wg, wu, wd):
    m, d = xg.shape
    rb = MOE_ROW_BLOCK
    f = wg.shape[2]
    tf = _largest_tile(f, 1792, LANES)
    return pl.pallas_call(
        _experts_kernel,
        grid_spec=pltpu.PrefetchScalarGridSpec(
            num_scalar_prefetch=2,
            grid=(m // rb, f // tf),
            in_specs=[
                pl.BlockSpec((rb, d), lambda i, j, be, act: (i, 0)),
                pl.BlockSpec((None, d, tf), lambda i, j, be, act: (be[i], 0, j)),
                pl.BlockSpec((None, d, tf), lambda i, j, be, act: (be[i], 0, j)),
                pl.BlockSpec((None, tf, d), lambda i, j, be, act: (be[i], j, 0)),
            ],
            out_specs=pl.BlockSpec((rb, d), lambda i, j, be, act: (i, 0)),
            scratch_shapes=[pltpu.VMEM((rb, d), F32)],
        ),
        out_shape=jax.ShapeDtypeStruct((m, d), F32),
        compiler_params=_cparams("arbitrary", "arbitrary"),
        name="moe_experts",
    )(block_expert, block_active, xg, wg, wu, wd)


def _combine_kernel(idx_ref, x_ref, yg_ref, g0_ref, g1_ref, o_ref, y_scr, sem):
    tm = x_ref.shape[0]

    def row_copy(r):
        return pltpu.make_async_copy(yg_ref.at[pl.ds(idx_ref[0, r], 1)], y_scr.at[pl.ds(r, 1)], sem)

    def start(i, c):
        row_copy(2 * i).start(priority=0)
        row_copy(2 * i + 1).start(priority=1)
        return c

    def wait(r, c):
        row_copy(r).wait()
        return c

    lax.fori_loop(0, TOP_K * tm // 2, start, 0, unroll=4)
    lax.fori_loop(0, TOP_K * tm, wait, 0, unroll=8)
    o_ref[...] = x_ref[...] + (g0_ref[...] * y_scr[0:tm, :] + g1_ref[...] * y_scr[tm:2 * tm, :])


def _combine(x, yg, slot, gates):
    n, d = x.shape
    tm = MOE_TOKEN_TILE
    nt = n // tm
    assert TOP_K == 2
    g0 = gates[0].reshape(n, 1)
    g1 = gates[1].reshape(n, 1)
    idx = jnp.transpose(slot.reshape(TOP_K, nt, tm), (1, 0, 2)).reshape(nt, 1, TOP_K * tm)
    return pl.pallas_call(
        _combine_kernel,
        grid=(nt,),
        in_specs=[pl.BlockSpec((None, 1, TOP_K * tm), lambda i: (i, 0, 0), memory_space=pltpu.SMEM),
                  pl.BlockSpec((tm, d), lambda i: (i, 0)), pl.BlockSpec(memory_space=pl.ANY),
                  pl.BlockSpec((tm, 1), lambda i: (i, 0)), pl.BlockSpec((tm, 1), lambda i: (i, 0))],
        out_specs=pl.BlockSpec((tm, d), lambda i: (i, 0)),
        out_shape=jax.ShapeDtypeStruct((n, d), F32),
        scratch_shapes=[pltpu.VMEM((TOP_K * tm, d), F32), pltpu.SemaphoreType.DMA(())],
        compiler_params=_cparams("arbitrary"),
        name="moe_combine",
    )(idx, x, yg, g0, g1)


def _moe(x, norm, router_t, wg, wu, wd):
    n = x.shape[0]
    rb = MOE_ROW_BLOCK
    h, idx, gates = _router(x, norm.reshape(1, -1), router_t)
    expert = idx.reshape(-1)
    n_asg = expert.shape[0]
    onehot = (expert[:, None] == jnp.arange(N_EXPERTS, dtype=jnp.int32)[None, :]).astype(jnp.int32)
    pos = jnp.take_along_axis(jnp.cumsum(onehot, axis=0), expert[:, None], axis=1)[:, 0] - 1
    counts = jnp.sum(onehot, axis=0)
    padded = (counts + rb - 1) // rb * rb
    p_end = jnp.cumsum(padded)
    slot = (p_end - padded)[expert] + pos
    n_blocks = -(-n_asg // rb) + N_EXPERTS
    token = jnp.tile(jnp.arange(n, dtype=jnp.int32), TOP_K)
    tok_of_slot = jnp.zeros((n_blocks * rb,), jnp.int32).at[slot].set(token)
    starts = jnp.arange(n_blocks, dtype=jnp.int32) * rb
    block_expert = jnp.minimum(jnp.searchsorted(p_end, starts, side="right"), N_EXPERTS - 1).astype(jnp.int32)
    block_active = (starts < p_end[-1]).astype(jnp.int32)
    xg = _gather_rows(h, tok_of_slot, rb)
    yg = _experts(block_expert, block_active, xg, wg, wu, wd)
    return _combine(x, yg, slot.astype(jnp.int32), gates)


def _col(v):
    return v.reshape(-1, 1).astype(F32)


def _gm_weights(w_in, q_norm, k_norm):
    wt = w_in.T
    n_ga0 = 2 * GLA_HEADS * GLA_DK + GLA_HEADS * GLA_DV
    wt = jnp.concatenate([wt[:n_ga0], wt[n_ga0 + GLA_GATE_RANK:], wt[n_ga0:n_ga0 + GLA_GATE_RANK],
                          jnp.zeros((LANES - GLA_GATE_RANK, wt.shape[1]), wt.dtype)], axis=0).astype(BF16)
    ones = lambda n: jnp.ones((n,), F32)
    nm = MOBA_HEADS * HEAD_DIM
    gain = jnp.concatenate([ones(256) * GLA_DK ** -0.5, ones(256), ones(512), ones(512),
                            jnp.tile(q_norm, MOBA_HEADS) * HEAD_DIM ** -0.5, jnp.tile(k_norm, MOBA_HEADS),
                            ones(nm), ones(LANES)])
    nflag = jnp.concatenate([jnp.zeros((1536,), F32), ones(2 * nm), jnp.zeros((nm + LANES,), F32)])
    return wt, _col(nflag), _col(gain)


GM_ROWS = dict(qg=0, kg=256, vg=512, rg=1024, qm=1536, km=2048, vm=2560, ga=3072)


def _gm_prompt(xp, bsz, seq, norm_mix, wts, gate_w, gate_b, out_norm, w_out, tm):
    wt, nflag, gain = wts
    r = GM_ROWS
    yt = _proj_t(xp, 0, bsz, seq, norm_mix.reshape(1, -1), wt, nflag, gain, _largest_tile(seq, WIDE_TOKEN_TILE, LANES))
    gw_t = gate_w.T.reshape(GLA_HEADS, GLA_DK, GLA_GATE_RANK)
    gb = gate_b.reshape(GLA_HEADS, GLA_DK, 1)
    on = out_norm.reshape(GLA_HEADS, GLA_DV, 1)
    og, s_t = _gla_prompt(yt, (r["qg"], r["kg"], r["vg"], r["rg"], r["ga"]), gw_t, gb, on)
    om = _moba_prompt(yt, _alibi_slopes(MOBA_HEADS), r["qm"], r["km"], r["vm"])
    xp = _outproj_t(xp, 0, [og, om], w_out.astype(BF16), tm)
    return xp, s_t, yt


NS_ROWS = dict(q=0, kc=1024, vc=1280, ks=1536, vs=1792, kw=2048, vw=2304, g=2560)


def _ns_weights(w_in, q_norm, ks_norm, kw_norm):
    wt = w_in.T
    n_g = NSA_HEADS * 3
    wt = jnp.concatenate([wt, jnp.zeros((LANES - n_g, wt.shape[1]), wt.dtype)], axis=0).astype(BF16)
    ones = lambda n: jnp.ones((n,), F32)
    zeros = lambda n: jnp.zeros((n,), F32)
    kvw = NSA_KV_HEADS * HEAD_DIM
    gain = jnp.concatenate([jnp.tile(q_norm, NSA_HEADS) * HEAD_DIM ** -0.5, ones(2 * kvw),
                            jnp.tile(ks_norm, NSA_KV_HEADS), ones(kvw),
                            jnp.tile(kw_norm, NSA_KV_HEADS), ones(kvw), ones(LANES)])
    nflag = jnp.concatenate([ones(NSA_HEADS * HEAD_DIM), zeros(2 * kvw), ones(kvw), zeros(kvw), ones(kvw),
                             zeros(kvw), zeros(LANES)])
    return wt, _col(nflag), _col(gain)


def _ns_prompt(xp, bsz, seq, norm_mix, wts, gate_b, kc_norm, cmpk, cmpv, w_out, tm):
    wt, nflag, gain = wts
    r = NS_ROWS
    yt = _proj_t(xp, 0, bsz, seq, norm_mix.reshape(1, -1), wt, nflag, gain, _largest_tile(seq, WIDE_TOKEN_TILE, LANES))
    kcmp_t =_cmp_stage2(_cmp_stage1_dense(yt, r["kc"], _cmp_w1cat(cmpk[1])), *cmpk, kc_norm)
    vcmp_t = _cmp_stage2(_cmp_stage1_dense(yt, r["vc"], _cmp_w1cat(cmpv[1])), *cmpv, None)
    gb = jnp.pad(gate_b, (0, LANES - gate_b.shape[0])).reshape(LANES, 1)
    o_t = _nsa_prompt(yt, kcmp_t, vcmp_t, _alibi_slopes(NSA_HEADS),
                      (r["q"], r["ks"], r["vs"], r["kw"], r["vw"], r["g"]), gb)
    xp = _outproj_t(xp, 0, [o_t], w_out.astype(BF16), tm)
    return xp, yt


def _sample_rows(xs, norm, wts):
    n = xs.shape[0]
    wt, nflag, gain = wts
    xs_pad = jnp.pad(xs, ((0, LANES - n), (0, 0)))
    yt = _proj_t(xs_pad, 0, 1, LANES, norm.reshape(1, -1), wt, nflag, gain, LANES)
    return yt[0, :, :n].T


def _ns_sample(xs, norm_mix, wts, gate_b, kc_norm, cmpk, cmpv, w_out, pools, win_k, win_v, page_table):
    n = xs.shape[0]
    r = NS_ROWS
    past = page_table.shape[1] * LANES
    ys = _sample_rows(xs, norm_mix, wts)
    kvw = NSA_KV_HEADS * HEAD_DIM
    new = {nm: ys[:, r[nm]:r[nm] + kvw].reshape(n, NSA_KV_HEADS, HEAD_DIM) for nm in ("kc", "vc", "ks", "vs", "kw", "vw")}
    colv = lambda a: a.reshape(n, NSA_KV_HEADS, HEAD_DIM, 1)
    q = ys[:, :NSA_HEADS * HEAD_DIM].reshape(n, NSA_KV_HEADS, NSA_GROUP, HEAD_DIM, 1)
    gt = ys[:, r["g"]:r["g"] + 3 * NSA_HEADS].reshape(n, NSA_KV_HEADS, 3 * NSA_GROUP, 1)
    gb = gate_b.reshape(NSA_KV_HEADS, 3 * NSA_GROUP, 1)
    view = lambda a: jnp.transpose(a, (0, 2, 3, 1))
    ck_t, cv_t, sk_t, sv_t = (view(p) for p in pools)
    kcmp_t = _cmp_stage2(_cmp_stage1_paged(page_table, ck_t, _cmp_w1cat(cmpk[1])), *cmpk, kc_norm)
    vcmp_t = _cmp_stage2(_cmp_stage1_paged(page_table, cv_t, _cmp_w1cat(cmpv[1])), *cmpv, None)
    slopes = _alibi_slopes(NSA_HEADS)
    o_c, sel = _nsa_sample_cmp(q, kcmp_t, vcmp_t, slopes, past, past + 1)
    sel = sel[:, :, 0, :NSA_SLC_TOPN].reshape(n, NSA_KV_HEADS * NSA_SLC_TOPN)
    rows = lambda a: a.reshape(n, NSA_KV_HEADS, -1, HEAD_DIM)
    o = _nsa_sample_attn(page_table, sel, slopes, rows(q), rows(o_c), gt, gb,
                         [rows(new[nm]) for nm in ("ks", "vs", "kw", "vw")], view(win_k), view(win_v), sk_t, sv_t)
    xs = _outproj_r(xs, o.reshape(n, NSA_HEADS * HEAD_DIM), w_out.astype(BF16))
    return xs, new


def _gm_sample(xs, norm_mix, wts, gate_w, gate_b, out_norm, w_out, state, pool_k, pool_v, page_table):
    n = xs.shape[0]
    r = GM_ROWS
    ys = _sample_rows(xs, norm_mix, wts)
    hk, hv, nm = GLA_HEADS * GLA_DK, GLA_HEADS * GLA_DV, MOBA_HEADS * HEAD_DIM
    col = lambda r0, heads, dim: ys[:, r0:r0 + heads * dim].reshape(n, heads, dim, 1)
    row = lambda r0, heads, dim: ys[:, r0:r0 + heads * dim].reshape(n, heads, 1, dim)
    gw_t = gate_w.T.reshape(GLA_HEADS, GLA_DK, GLA_GATE_RANK)
    og, s_new = _gla_sample(
        col(r["qg"], GLA_HEADS, GLA_DK), col(r["kg"], GLA_HEADS, GLA_DK), row(r["vg"], GLA_HEADS, GLA_DV),
        row(r["rg"], GLA_HEADS, GLA_DV), ys[:, r["ga"]:r["ga"] + GLA_GATE_RANK].reshape(n, 1, 1, GLA_GATE_RANK),
        gw_t, gate_b.reshape(GLA_HEADS, GLA_DK, 1), out_norm.reshape(GLA_HEADS, 1, GLA_DV), state)
    qm, km, vm = (col(r[nme], MOBA_HEADS, HEAD_DIM) for nme in ("qm", "km", "vm"))
    kpool_t = jnp.transpose(pool_k, (0, 2, 3, 1))
    vpool_t = jnp.transpose(pool_v, (0, 2, 3, 1))
    _, top = _moba_gates(page_table, qm, kpool_t)
    top = top[:, :, :MOBA_TOPK].reshape(n, MOBA_HEADS * MOBA_TOPK)
    om = _moba_sample(page_table, top, _alibi_slopes(MOBA_HEADS), qm, km, vm, kpool_t, vpool_t)
    a = jnp.concatenate([og.reshape(n, hv), om.reshape(n, nm)], axis=1)
    xs = _outproj_r(xs, a, w_out.astype(BF16))
    return xs, s_new, km.reshape(n, MOBA_HEADS, HEAD_DIM), vm.reshape(n, MOBA_HEADS, HEAD_DIM)


def _rows_to_cache(yt, row0, heads):
    bsz, _, seq = yt.shape
    a = yt[:, row0:row0 + heads * HEAD_DIM].reshape(bsz, heads, HEAD_DIM, seq)
    return jnp.transpose(a, (0, 3, 1, 2))


def kernel(x_prompt, x_sample, state_gla, cache_moba_k, cache_moba_v, cache_nsa_cmp_k, cache_nsa_cmp_v, cache_nsa_slc_k, cache_nsa_slc_v, state_nsa_win_k, state_nsa_win_v, page_table, gm_norm_mix, gm_w_in, gm_gla_gate_w, gm_gla_gate_b, gm_gla_out_norm, gm_moba_q_norm, gm_moba_k_norm, gm_w_out, gm_norm_ffn, gm_ffn_gate, gm_ffn_up, gm_ffn_down, ns_norm_mix, ns_w_in, ns_gate_b, ns_q_norm, ns_kcmp_norm, ns_kslc_norm, ns_kwin_norm, ns_cmpk_pe, ns_cmpk_w1, ns_cmpk_b1, ns_cmpk_w2, ns_cmpk_b2, ns_cmpv_pe, ns_cmpv_w1, ns_cmpv_b1, ns_cmpv_w2, ns_cmpv_b2, ns_w_out, ns_norm_ffn, ns_router, ns_exp_gate, ns_exp_up, ns_exp_down):
    bsz, seq, d = x_prompt.shape
    n_s = x_sample.shape[0]
    assert x_sample.shape[1] == 1 and gm_w_in.shape[0] == 1 and ns_w_in.shape[0] == 1
    tm = _largest_tile(seq, 512, LANES)
    xp = x_prompt.reshape(bsz * seq, d)
    xs = x_sample.reshape(n_s, d)

    i = 0
    wts = _gm_weights(gm_w_in[i], gm_moba_q_norm[i], gm_moba_k_norm[i])
    xp, gla_p, yt0 = _gm_prompt(xp, bsz, seq, gm_norm_mix[i], wts, gm_gla_gate_w[i], gm_gla_gate_b[i],
                                gm_gla_out_norm[i], gm_w_out[i], tm)
    xs, gla_s, mk_s, mv_s = _gm_sample(xs, gm_norm_mix[i], wts, gm_gla_gate_w[i], gm_gla_gate_b[i],
                                       gm_gla_out_norm[i], gm_w_out[i], state_gla[i], cache_moba_k[i],
                                       cache_moba_v[i], page_table)
    ffn = (gm_norm_ffn[i].reshape(1, -1), gm_ffn_gate[i].astype(BF16), gm_ffn_up[i].astype(BF16),
           gm_ffn_down[i].astype(BF16))
    xp = _ffn(xp, *ffn, _largest_tile(seq, WIDE_TOKEN_TILE, LANES))
    xs = _ffn(xs, *ffn, n_s)

    wts = _ns_weights(ns_w_in[i], ns_q_norm[i], ns_kslc_norm[i], ns_kwin_norm[i])
    cmpk = (ns_cmpk_pe[i], ns_cmpk_w1[i], ns_cmpk_b1[i], ns_cmpk_w2[i], ns_cmpk_b2[i])
    cmpv = (ns_cmpv_pe[i], ns_cmpv_w1[i], ns_cmpv_b1[i], ns_cmpv_w2[i], ns_cmpv_b2[i])
    xp, yt1 = _ns_prompt(xp, bsz, seq, ns_norm_mix[i], wts, ns_gate_b[i], ns_kcmp_norm[i], cmpk, cmpv,
                         ns_w_out[i], tm)
    pools = (cache_nsa_cmp_k[i], cache_nsa_cmp_v[i], cache_nsa_slc_k[i], cache_nsa_slc_v[i])
    xs, new = _ns_sample(xs, ns_norm_mix[i], wts, ns_gate_b[i], ns_kcmp_norm[i], cmpk, cmpv, ns_w_out[i], pools,
                         state_nsa_win_k[i], state_nsa_win_v[i], page_table)
    n_tok = bsz * seq + n_s
    n_pad = -(-n_tok // MOE_TOKEN_TILE) * MOE_TOKEN_TILE
    x_all = jnp.concatenate([xp, xs, jnp.zeros((n_pad - n_tok, d), F32)], axis=0)
    x_all = _moe(x_all, ns_norm_ffn[i], ns_router[i].T, ns_exp_gate[i].astype(BF16), ns_exp_up[i].astype(BF16),
                 ns_exp_down[i].astype(BF16))
    y_prompt = x_all[:bsz * seq].reshape(bsz, seq, d)
    y_sample = x_all[bsz * seq:n_tok].reshape(n_s, 1, d)

    r0, r1 = GM_ROWS, NS_ROWS
    n_win = min(NSA_WINDOW, seq)
    ns_p = {nm: _rows_to_cache(yt1, r1[nm], NSA_KV_HEADS) for nm in ("kc", "vc", "ks", "vs", "kw", "vw")}
    win_s = lambda state, row: jnp.concatenate([state, row[:, None]], axis=1)[:, -min(NSA_WINDOW, state.shape[1] + 1):]
    lead = lambda a: a[None]
    tok = lambda a: a[None, :, None]
    return (y_prompt, y_sample, lead(gla_p), lead(gla_s),
            lead(_rows_to_cache(yt0, r0["km"], MOBA_HEADS)), lead(_rows_to_cache(yt0, r0["vm"], MOBA_HEADS)),
            tok(mk_s), tok(mv_s),
            lead(ns_p["kc"]), lead(ns_p["vc"]), lead(ns_p["ks"]), lead(ns_p["vs"]),
            lead(ns_p["kw"][:, -n_win:]), lead(ns_p["vw"][:, -n_win:]),
            tok(new["kc"]), tok(new["vc"]), tok(new["ks"]), tok(new["vs"]),
            lead(win_s(state_nsa_win_k[i], new["kw"])), lead(win_s(state_nsa_win_v[i], new["vw"])))
```
